```python
import math
import jax
import jax.numpy as jnp
from jax import lax
import numpy as np

D_MODEL = 1024
BATCH = 8
SEQ = 4096
DEPTH = 4

GRID_W = 64
CTX_LEN = 256
NA_HEAD_DIM = 64
NA_HEADS = (D_MODEL // 2) // NA_HEAD_DIM
NA_W = NA_HEADS * NA_HEAD_DIM
NA_WIN_R = 8
NA_WIN_C = 16
GDN_HEAD_DIM = 128
GDN_HEADS = (D_MODEL - NA_W) // GDN_HEAD_DIM
GDN_W = GDN_HEADS * GDN_HEAD_DIM
MIX_W = NA_W + GDN_W
GDN_CONV = 5
GDN_CHUNK = 64
ROPE_BASE = 10000.0
D_FF = 4 * D_MODEL
IN_COLS = 3 * NA_W + 4 * GDN_W + 4 * GDN_HEADS
N_MOD = 6
LN_EPS = 1e-5
NORM_EPS = 1e-6
DEEPNORM_ALPHA = (2 * DEPTH) ** 0.25
DEEPNORM_BETA = (8 * DEPTH) ** -0.25

kernel_name = 'hybrid_na_gdn_diffusion_block'


def layer_norm(x, g, b):
    xf = x.astype(jnp.float32)
    mu = jnp.mean(xf, axis=-1, keepdims=True)
    var = jnp.mean(jnp.square(xf - mu), axis=-1, keepdims=True)
    y = (xf - mu) * lax.rsqrt(var + LN_EPS) * g.astype(jnp.float32) + b.astype(jnp.float32)
    return y.astype(x.dtype)


def l2norm(x):
    xf = x.astype(jnp.float32)
    return (xf * lax.rsqrt(jnp.sum(jnp.square(xf), axis=-1, keepdims=True) + NORM_EPS)).astype(x.dtype)


def gated_rmsnorm(o, z, w):
    of = o.astype(jnp.float32)
    of = of * lax.rsqrt(jnp.mean(jnp.square(of), axis=-1, keepdims=True) + NORM_EPS) * w.astype(jnp.float32)
    y = of.astype(o.dtype) * jax.nn.silu(z)
    return y.reshape(o.shape[0], o.shape[1], -1)


def axial_rope_tables(t):
    pos = jnp.arange(t)
    row = (pos // GRID_W).astype(jnp.float32)
    col = (pos % GRID_W).astype(jnp.float32)
    axis_dim = GDN_HEAD_DIM // 2
    inv_freq = ROPE_BASE ** (-jnp.arange(0, axis_dim, 2, dtype=jnp.float32) / axis_dim)
    ang_r = row[:, None] * inv_freq[None, :]
    ang_c = col[:, None] * inv_freq[None, :]
    return (jnp.cos(ang_r), jnp.sin(ang_r), jnp.cos(ang_c), jnp.sin(ang_c))


def rope_axis(x, cos, sin):
    x1, x2 = jnp.split(x, 2, axis=-1)
    cos = cos[:, None, :].astype(x.dtype)
    sin = sin[:, None, :].astype(x.dtype)
    return jnp.concatenate([x1 * cos - x2 * sin, x2 * cos + x1 * sin], axis=-1)


def rope_2d(x, rope):
    cos_r, sin_r, cos_c, sin_c = rope
    x_row, x_col = jnp.split(x, 2, axis=-1)
    return jnp.concatenate([rope_axis(x_row, cos_r, sin_r), rope_axis(x_col, cos_c, sin_c)], axis=-1)


def centred_conv(x, w):
    k = w.shape[0]
    return lax.conv_general_dilated(
        x, w[:, None, :], window_strides=(1,), padding=[(k // 2, k // 2)],
        dimension_numbers=('NWC', 'WIO', 'NWC'), feature_group_count=x.shape[-1])


def gdn_inputs(qkv, beta_raw, a_raw, conv_w, a_log, dt_bias, rope):
    b, t, _ = qkv.shape
    qkv = jax.nn.silu(centred_conv(qkv, conv_w))
    q, k, v = [u.reshape(b, t, GDN_HEADS, GDN_HEAD_DIM) for u in jnp.split(qkv, 3, axis=-1)]
    q, k = l2norm(q), l2norm(k)
    if rope is not None:
        q, k = rope_2d(q, rope), rope_2d(k, rope)
    q = q * (GDN_HEAD_DIM ** -0.5)
    beta = jax.nn.sigmoid(beta_raw).reshape(b, t, 2, GDN_HEADS)
    a = a_raw.reshape(b, t, 2, GDN_HEADS).astype(jnp.float32)
    g = -jnp.exp(a_log.astype(jnp.float32)) * jax.nn.softplus(a + dt_bias.astype(jnp.float32))
    return q, k, v, beta, g


def unit_lower_inverse(a):
    c = a.shape[-1]
    n = -a
    inv = jnp.eye(c, dtype=a.dtype) + n
    p = n
    for _ in range(int(math.log2(c)) - 1):
        p = p @ p
        inv = inv + inv @ p
    return inv


def gated_delta_chunked(q, k, v, g, beta, s0):
    b, t, h, _ = q.shape
    dv = v.shape[-1]
    c = GDN_CHUNK
    n = t // c

    def to_chunks(u):
        u = u.reshape((b, n, c, h) + u.shape[3:])
        return jnp.moveaxis(u, (1, 3), (0, 2))

    qc, kc, vc, bc = to_chunks(q), to_chunks(k), to_chunks(v), to_chunks(beta)
    gc = jnp.cumsum(to_chunks(g).astype(jnp.float32), axis=-1)
    idx = jnp.arange(c)
    lower = idx[:, None] >= idx[None, :]
    strict = idx[:, None] > idx[None, :]
    decay = jnp.exp(jnp.where(lower, gc[..., :, None] - gc[..., None, :], -jnp.inf)).astype(q.dtype)
    kb = kc * bc[..., None]
    a = jnp.where(strict, jnp.einsum('nbhid,nbhjd->nbhij', kb, kc) * decay, 0.0).astype(q.dtype)
    tinv = unit_lower_inverse(a)
    eg = jnp.exp(gc).astype(q.dtype)
    u = tinv @ (vc * bc[..., None])
    w = tinv @ (kb * eg[..., None])
    attn = jnp.einsum('nbhid,nbhjd->nbhij', qc, kc) * decay
    q_dec = qc * eg[..., None]
    k_dec = kc * jnp.exp(gc[..., -1:] - gc)[..., None].astype(q.dtype)
    chunk_decay = jnp.exp(gc[..., -1]).astype(q.dtype)

    def step(s, xs):
        u_i, w_i, attn_i, qd_i, kd_i, cd_i = xs
        v_new = u_i - jnp.einsum('bhck,bhkv->bhcv', w_i, s)
        o_i = jnp.einsum('bhck,bhkv->bhcv', qd_i, s) + jnp.einsum('bhij,bhjv->bhiv', attn_i, v_new)
        s = s * cd_i[..., None, None] + jnp.einsum('bhck,bhcv->bhkv', kd_i, v_new)
        return s, o_i

    s_final, o = lax.scan(step, s0, (u, w, attn, q_dec, k_dec, chunk_decay))
    o = jnp.moveaxis(o, (0, 2), (1, 3)).reshape(b, t, h, dv)
    return o, s_final


def neighbourhood_attention(q, k, v, k_ctx, v_ctx, rpb, rows):
    b, t, h, d = q.shape
    win_r = min(NA_WIN_R, rows)
    n_band = win_r * GRID_W
    cols = jnp.arange(GRID_W)
    col_start = jnp.clip(cols - NA_WIN_C // 2, 0, GRID_W - NA_WIN_C)
    col_in = (cols[None, :] >= col_start[:, None]) & (cols[None, :] < col_start[:, None] + NA_WIN_C)
    dc_idx = jnp.clip(cols[None, :] - cols[:, None], 1 - NA_WIN_C, NA_WIN_C - 1) + NA_WIN_C - 1
    rpb_cols = rpb[:, :, dc_idx].astype(jnp.float32)
    q = q * (d ** -0.5)

    def row_block(r):
        row_start = jnp.clip(r - win_r // 2, 0, rows - win_r)
        q_r = lax.dynamic_slice_in_dim(q, r * GRID_W, GRID_W, axis=1)
        k_band = lax.dynamic_slice_in_dim(k, row_start * GRID_W, n_band, axis=1)
        v_band = lax.dynamic_slice_in_dim(v, row_start * GRID_W, n_band, axis=1)
        dr_idx = row_start + jnp.arange(win_r) - r + NA_WIN_R - 1
        bias = jnp.take(rpb_cols, dr_idx, axis=1)
        bias = jnp.where(col_in[None, None], bias, -jnp.inf)
        bias = jnp.transpose(bias, (0, 2, 1, 3)).reshape(h, GRID_W, n_band)
        s_lat = jnp.einsum('bqhd,bkhd->bhqk', q_r, k_band).astype(jnp.float32) + bias
        s_ctx = jnp.einsum('bqhd,bkhd->bhqk', q_r, k_ctx).astype(jnp.float32)
        p = jax.nn.softmax(jnp.concatenate([s_lat, s_ctx], axis=-1), axis=-1).astype(v.dtype)
        return (jnp.einsum('bhqk,bkhd->bqhd', p[..., :n_band], v_band)
                + jnp.einsum('bhqk,bkhd->bqhd', p[..., n_band:], v_ctx))

    o = lax.map(row_block, jnp.arange(rows))
    return jnp.moveaxis(o, 0, 1).reshape(b, t, h * d)


def context_attention(q, k, v):
    b, l, h, d = q.shape
    s = jnp.einsum('bqhd,bkhd->bhqk', q * (d ** -0.5), k).astype(jnp.float32)
    p = jax.nn.softmax(s, axis=-1).astype(v.dtype)
    return jnp.einsum('bhqk,bkhd->bqhd', p, v).reshape(b, l, h * d)


def token_mixer(h_lat, h_ctx, w_in, conv_w, a_log, dt_bias, gdn_norm_w, rpb, w_out, rope, with_ctx_out):
    b, t, _ = h_lat.shape
    rows = t // GRID_W
    cuts = [NA_W, 2 * NA_W, 3 * NA_W, 3 * NA_W + 3 * GDN_W, 3 * NA_W + 4 * GDN_W,
            3 * NA_W + 4 * GDN_W + 2 * GDN_HEADS]
    nq, nk, nv, qkv, z, beta_raw, a_raw = jnp.split(h_lat @ w_in, cuts, axis=-1)
    cq, ck, cv, cqkv, cz, cbeta_raw, ca_raw = jnp.split(h_ctx @ w_in, cuts, axis=-1)

    def heads(u, nh, hd):
        return u.reshape(u.shape[0], u.shape[1], nh, hd)

    ck_h, cv_h = heads(ck, NA_HEADS, NA_HEAD_DIM), heads(cv, NA_HEADS, NA_HEAD_DIM)
    na_lat = neighbourhood_attention(heads(nq, NA_HEADS, NA_HEAD_DIM), heads(nk, NA_HEADS, NA_HEAD_DIM),
                                     heads(nv, NA_HEADS, NA_HEAD_DIM), ck_h, cv_h, rpb, rows)

    q, k, v, beta, g = gdn_inputs(qkv, beta_raw, a_raw, conv_w, a_log, dt_bias, rope)
    cq_g, ck_g, cv_g, cbeta, cg = gdn_inputs(cqkv, cbeta_raw, ca_raw, conv_w, a_log, dt_bias, None)
    zeros = jnp.zeros((b, GDN_HEADS, GDN_HEAD_DIM, GDN_HEAD_DIM), q.dtype)

    def flip(u):
        return jnp.flip(u, axis=1)

    co_f, s_f = gated_delta_chunked(cq_g, ck_g, cv_g, cg[:, :, 0], cbeta[:, :, 0], zeros)
    co_b, s_b = gated_delta_chunked(flip(cq_g), flip(ck_g), flip(cv_g), flip(cg[:, :, 1]), flip(cbeta[:, :, 1]), zeros)
    o_f, _ = gated_delta_chunked(q, k, v, g[:, :, 0], beta[:, :, 0], s_f)
    o_b, _ = gated_delta_chunked(flip(q), flip(k), flip(v), flip(g[:, :, 1]), flip(beta[:, :, 1]), s_b)
    gdn_lat = gated_rmsnorm(o_f + flip(o_b), heads(z, GDN_HEADS, GDN_HEAD_DIM), gdn_norm_w)

    y_lat = jnp.concatenate([na_lat, gdn_lat], axis=-1) @ w_out
    if not with_ctx_out:
        return y_lat, None
    na_ctx = context_attention(heads(cq, NA_HEADS, NA_HEAD_DIM), ck_h, cv_h)
    gdn_ctx = gated_rmsnorm(co_f + flip(co_b), heads(cz, GDN_HEADS, GDN_HEAD_DIM), gdn_norm_w)
    y_ctx = jnp.concatenate([na_ctx, gdn_ctx], axis=-1) @ w_out
    return y_lat, y_ctx


def squared_relu_mlp(h, w1, w2):
    return jnp.square(jax.nn.relu(h @ w1)) @ w2


def setup_inputs(seed: int = 0) -> dict:
    key = jax.random.key(seed)
    ks = jax.random.split(key, 20)
    f32 = jnp.float32
    nrm = lambda k, s: jax.random.normal(k, s, f32)
    x = nrm(ks[0], (BATCH, SEQ, D_MODEL))
    c = nrm(ks[1], (BATCH, D_MODEL))
    ctx = nrm(ks[2], (BATCH, CTX_LEN, D_MODEL))
    c_ctx = nrm(ks[3], (D_MODEL,))
    w_ada = nrm(ks[4], (DEPTH, D_MODEL, N_MOD * D_MODEL)) * (0.5 * D_MODEL ** -0.5)
    b_ada = 0.02 * nrm(ks[5], (DEPTH, N_MOD * D_MODEL))
    w_in = nrm(ks[6], (DEPTH, D_MODEL, IN_COLS)) * D_MODEL ** -0.5
    conv_w = nrm(ks[7], (DEPTH, GDN_CONV, 3 * GDN_W)) * GDN_CONV ** -0.5
    a_log = jnp.log(jax.random.uniform(ks[8], (DEPTH, 2, GDN_HEADS), f32, minval=1.0, maxval=16.0))
    dt = jnp.exp(jax.random.uniform(ks[9], (DEPTH, 2, GDN_HEADS), f32,
                                    minval=math.log(1e-3), maxval=math.log(1e-1)))
    dt_bias = dt + jnp.log(-jnp.expm1(-dt))
    gdn_norm_w = 1.0 + 0.02 * nrm(ks[10], (DEPTH, GDN_HEAD_DIM))
    rpb = 0.02 * nrm(ks[11], (DEPTH, NA_HEADS, 2 * NA_WIN_R - 1, 2 * NA_WIN_C - 1))
    w_out = nrm(ks[12], (DEPTH, MIX_W, D_MODEL)) * (MIX_W ** -0.5 * DEEPNORM_BETA)
    ln1_g = 1.0 + 0.02 * nrm(ks[13], (DEPTH, D_MODEL))
    ln1_b = 0.02 * nrm(ks[14], (DEPTH, D_MODEL))
    w_mlp1 = nrm(ks[15], (DEPTH, D_MODEL, D_FF)) * D_MODEL ** -0.5
    w_mlp2 = nrm(ks[16], (DEPTH, D_FF, D_MODEL)) * (D_FF ** -0.5 * DEEPNORM_BETA)
    ln2_g = 1.0 + 0.02 * nrm(ks[17], (DEPTH, D_MODEL))
    ln2_b = 0.02 * nrm(ks[18], (DEPTH, D_MODEL))
    return {'x': x, 'c': c, 'ctx': ctx, 'c_ctx': c_ctx, 'w_ada': w_ada, 'b_ada': b_ada, 'w_in': w_in,
            'conv_w': conv_w, 'a_log': a_log, 'dt_bias': dt_bias, 'gdn_norm_w': gdn_norm_w, 'rpb': rpb,
            'w_out': w_out, 'ln1_g': ln1_g, 'ln1_b': ln1_b, 'w_mlp1': w_mlp1, 'w_mlp2': w_mlp2,
            'ln2_g': ln2_g, 'ln2_b': ln2_b}


def reference(x, c, ctx, c_ctx, w_ada, b_ada, w_in, conv_w, a_log, dt_bias, gdn_norm_w, rpb,
              w_out, ln1_g, ln1_b, w_mlp1, w_mlp2, ln2_g, ln2_b):
    rope = axial_rope_tables(x.shape[1])
    silu_c = jax.nn.silu(c)
    silu_cc = jax.nn.silu(c_ctx)
    x_lat, x_ctx = x, ctx
    for l in range(DEPTH):
        ctx_out = l < DEPTH - 1
        sh1, sc1, g1, sh2, sc2, g2 = jnp.split((silu_c @ w_ada[l] + b_ada[l])[:, None, :], N_MOD, axis=-1)
        csh1, csc1, cg1, csh2, csc2, cg2 = jnp.split(silu_cc @ w_ada[l] + b_ada[l], N_MOD, axis=-1)
        y_lat, y_ctx = token_mixer(x_lat * (1.0 + sc1) + sh1, x_ctx * (1.0 + csc1) + csh1, w_in[l], conv_w[l],
                                   a_log[l], dt_bias[l], gdn_norm_w[l], rpb[l], w_out[l], rope, ctx_out)
        x_lat = layer_norm(DEEPNORM_ALPHA * x_lat + g1 * y_lat, ln1_g[l], ln1_b[l])
        x_lat = layer_norm(DEEPNORM_ALPHA * x_lat
                           + g2 * squared_relu_mlp(x_lat * (1.0 + sc2) + sh2, w_mlp1[l], w_mlp2[l]),
                           ln2_g[l], ln2_b[l])
        if ctx_out:
            x_ctx = layer_norm(DEEPNORM_ALPHA * x_ctx + cg1 * y_ctx, ln1_g[l], ln1_b[l])
            x_ctx = layer_norm(DEEPNORM_ALPHA * x_ctx
                               + cg2 * squared_relu_mlp(x_ctx * (1.0 + csc2) + csh2, w_mlp1[l], w_mlp2[l]),
                               ln2_g[l], ln2_b[l])
    return x_lat
```

```python
import functools
import math

import jax
import jax.numpy as jnp
from jax import lax
from jax.experimental import pallas as pl
from jax.experimental.pallas import tpu as pltpu

GRID_W = 64
NA_HEAD_DIM = 64
NA_WIN_R = 8
NA_WIN_C = 16
GDN_HEAD_DIM = 128
GDN_CHUNK = 64
ROPE_BASE = 10000.0
N_MOD = 6
LN_EPS = 1e-5
NORM_EPS = 1e-6
MOD_ROWS = 16
BA_PAD = 128
VMEM_LIMIT = 56 * 1024 * 1024

BF16 = jnp.bfloat16
F32 = jnp.float32


def _params(*sem):
    return pltpu.CompilerParams(dimension_semantics=sem, vmem_limit_bytes=VMEM_LIMIT)


def _dot(a, b):
    return jnp.dot(a, b, preferred_element_type=F32)


def _dot_nt(a, b):
    return lax.dot_general(a, b, (((1,), (1,)), ((), ())), preferred_element_type=F32)


def _dot_tn(a, b):
    return lax.dot_general(a, b, (((0,), (0,)), ((), ())), preferred_element_type=F32)


def _silu(x):
    return x * jax.nn.sigmoid(x)


def _softplus(x):
    return jnp.maximum(x, 0.0) + jnp.log1p(jnp.exp(-jnp.abs(x)))


def _layer_norm(x, g, b):
    mu = jnp.mean(x, axis=-1, keepdims=True)
    xc = x - mu
    var = jnp.mean(xc * xc, axis=-1, keepdims=True)
    return xc * lax.rsqrt(var + LN_EPS) * g + b


def _ada_kernel(c_ref, w_ref, b_ref, o_ref):
    o_ref[...] = _dot(_silu(c_ref[...]), w_ref[...]) + b_ref[...]


def _ada_call(cc, w_ada, b_ada):
    depth, d, n = w_ada.shape
    tn = 1536
    return pl.pallas_call(
        _ada_kernel,
        grid=(depth, n // tn),
        in_specs=[pl.BlockSpec((MOD_ROWS, d), lambda l, j: (0, 0)),
                  pl.BlockSpec((None, d, tn), lambda l, j: (l, 0, j)),
                  pl.BlockSpec((None, 1, tn), lambda l, j: (l, 0, j))],
        out_specs=pl.BlockSpec((None, MOD_ROWS, tn), lambda l, j: (l, 0, j)),
        out_shape=jax.ShapeDtypeStruct((depth, MOD_ROWS, n), F32),
        compiler_params=_params("parallel", "parallel"),
        name="ada_mod",
    )(cc, w_ada, b_ada.reshape(depth, 1, n))


def _inproj_kernel(x_ref, mod_ref, w_ref, na_ref, g_ref, z_ref, ba_ref, *, na3, g3, zw):
    h = x_ref[...] * (1.0 + mod_ref[1:2, :]) + mod_ref[0:1, :]
    hb = h.astype(BF16)
    na_ref[...] = _dot(hb, w_ref[:, 0:na3]).astype(na_ref.dtype)
    g_ref[...] = _dot(hb, w_ref[:, na3:na3 + g3])
    z_ref[...] = _dot(hb, w_ref[:, na3 + g3:na3 + g3 + zw])
    ba_ref[...] = _dot(hb, w_ref[:, na3 + g3 + zw:])


def _inproj_call(x, mods, w_in_b, layer, mod_row, tm):
    b, t, d = x.shape
    na3, g3, zw = 3 * (d // 2), 3 * (d // 2), d // 2
    ncols = w_in_b.shape[-1]
    mod_idx = (lambda bi, ti: (layer, bi, 0, 0)) if mod_row is None else (lambda bi, ti: (layer, mod_row, 0, 0))
    return pl.pallas_call(
        functools.partial(_inproj_kernel, na3=na3, g3=g3, zw=zw),
        grid=(b, t // tm),
        in_specs=[pl.BlockSpec((None, tm, d), lambda bi, ti: (bi, ti, 0)),
                  pl.BlockSpec((None, None, N_MOD, d), mod_idx),
                  pl.BlockSpec((None, d, ncols), lambda bi, ti: (layer, 0, 0))],
        out_specs=[pl.BlockSpec((None, tm, na3), lambda bi, ti: (bi, ti, 0)),
                   pl.BlockSpec((None, tm, g3), lambda bi, ti: (bi, ti, 0)),
                   pl.BlockSpec((None, tm, zw), lambda bi, ti: (bi, ti, 0)),
                   pl.BlockSpec((None, tm, BA_PAD), lambda bi, ti: (bi, ti, 0))],
        out_shape=[jax.ShapeDtypeStruct((b, t, na3), BF16),
                   jax.ShapeDtypeStruct((b, t, g3), F32),
                   jax.ShapeDtypeStruct((b, t, zw), F32),
                   jax.ShapeDtypeStruct((b, t, BA_PAD), F32)],
        compiler_params=_params("parallel", "parallel"),
        name="in_proj",
    )(x, mods, w_in_b)


def _softmax_pv(s_list, v_list):
    m = None
    for s in s_list:
        sm = jnp.max(s, axis=-1, keepdims=True)
        m = sm if m is None else jnp.maximum(m, sm)
    den = None
    acc = None
    for s, v in zip(s_list, v_list):
        p = jnp.exp(s - m)
        ps = jnp.sum(p, axis=-1, keepdims=True)
        den = ps if den is None else den + ps
        pv = _dot(p.astype(BF16), v)
        acc = pv if acc is None else acc + pv
    return acc / den


def _na_kernel(q_ref, k_ref, v_ref, kc_ref, vc_ref, bias_ref, o_ref, *, heads, rows, win_r):
    r = pl.program_id(1)
    n_band = win_r * GRID_W
    row_start = jnp.clip(r - win_r // 2, 0, rows - win_r)
    koff = pl.multiple_of(row_start * GRID_W, GRID_W)
    scale = NA_HEAD_DIM ** -0.5
    outs = []
    for h in range(heads):
        sl = slice(h * NA_HEAD_DIM, (h + 1) * NA_HEAD_DIM)
        qh = q_ref[:, sl]
        s_lat = _dot_nt(qh, k_ref[pl.ds(koff, n_band), sl]) * scale + bias_ref[h]
        s_ctx = _dot_nt(qh, kc_ref[:, sl]) * scale
        outs.append(_softmax_pv([s_lat, s_ctx], [v_ref[pl.ds(koff, n_band), sl], vc_ref[:, sl]]))
    o_ref[...] = jnp.concatenate(outs, axis=-1).astype(o_ref.dtype)


def _na_bias_table(rpb, win_r):
    cols = jnp.arange(GRID_W)
    col_start = jnp.clip(cols - NA_WIN_C // 2, 0, GRID_W - NA_WIN_C)
    col_in = (cols[None, :] >= col_start[:, None]) & (cols[None, :] < col_start[:, None] + NA_WIN_C)
    dc_idx = jnp.clip(cols[None, :] - cols[:, None], 1 - NA_WIN_C, NA_WIN_C - 1) + NA_WIN_C - 1
    rpb_cols = rpb[:, :, dc_idx].astype(F32)
    rpb_cols = jnp.where(col_in[None, None], rpb_cols, -jnp.inf)
    h = rpb.shape[0]
    variants = []
    for dv in range(win_r):
        dr0 = dv + NA_WIN_R - win_r
        band = rpb_cols[:, dr0:dr0 + win_r]
        variants.append(jnp.transpose(band, (0, 2, 1, 3)).reshape(h, GRID_W, win_r * GRID_W))
    return jnp.stack(variants)


def _na_call(na_lat, na_ctx, bias_tab):
    b, t, w3 = na_lat.shape
    w = w3 // 3
    heads = w // NA_HEAD_DIM
    l = na_ctx.shape[1]
    rows = t // GRID_W
    win_r = min(NA_WIN_R, rows)
    n_band = win_r * GRID_W

    def bias_idx(bi, r):
        row_start = jnp.clip(r - win_r // 2, 0, rows - win_r)
        return (row_start - r + win_r - 1, 0, 0, 0)

    return pl.pallas_call(
        functools.partial(_na_kernel, heads=heads, rows=rows, win_r=win_r),
        grid=(b, rows),
        in_specs=[pl.BlockSpec((None, GRID_W, w), lambda bi, r: (bi, r, 0)),
                  pl.BlockSpec((None, t, w), lambda bi, r: (bi, 0, 1)),
                  pl.BlockSpec((None, t, w), lambda bi, r: (bi, 0, 2)),
                  pl.BlockSpec((None, l, w), lambda bi, r: (bi, 0, 1)),
                  pl.BlockSpec((None, l, w), lambda bi, r: (bi, 0, 2)),
                  pl.BlockSpec((None, heads, GRID_W, n_band), bias_idx)],
        out_specs=pl.BlockSpec((None, GRID_W, w), lambda bi, r: (bi, r, 0)),
        out_shape=jax.ShapeDtypeStruct((b, t, w), BF16),
        compiler_params=_params("parallel", "arbitrary"),
        name="na_attn",
    )(na_lat, na_lat, na_lat, na_ctx, na_ctx, bias_tab)


def _ctx_attn_kernel(q_ref, k_ref, v_ref, o_ref, *, heads):
    scale = NA_HEAD_DIM ** -0.5
    outs = []
    for h in range(heads):
        sl = slice(h * NA_HEAD_DIM, (h + 1) * NA_HEAD_DIM)
        s = _dot_nt(q_ref[:, sl], k_ref[:, sl]) * scale
        outs.append(_softmax_pv([s], [v_ref[:, sl]]))
    o_ref[...] = jnp.concatenate(outs, axis=-1).astype(o_ref.dtype)


def _ctx_attn_call(na_ctx):
    b, l, w3 = na_ctx.shape
    w = w3 // 3
    return pl.pallas_call(
        functools.partial(_ctx_attn_kernel, heads=w // NA_HEAD_DIM),
        grid=(b,),
        in_specs=[pl.BlockSpec((None, l, w), lambda bi: (bi, 0, 0)),
                  pl.BlockSpec((None, l, w), lambda bi: (bi, 0, 1)),
                  pl.BlockSpec((None, l, w), lambda bi: (bi, 0, 2))],
        out_specs=pl.BlockSpec((None, l, w), lambda bi: (bi, 0, 0)),
        out_shape=jax.ShapeDtypeStruct((b, l, w), BF16),
        compiler_params=_params("parallel"),
        name="ctx_attn",
    )(na_ctx, na_ctx, na_ctx)


def _conv_silu(prev_ref, x_ref, next_ref, w_ref, pad_ref, tt):
    ti = pl.program_id(2)
    nt = pl.num_programs(2)
    kw = w_ref.shape[0]
    half = kw // 2
    pad_ref[0:8, :] = jnp.where(ti > 0, prev_ref[...], 0.0)
    pad_ref[8 + tt:16 + tt, :] = jnp.where(ti < nt - 1, next_ref[...], 0.0)
    pad_ref[8:8 + tt, :] = x_ref[...]
    acc = None
    for j in range(kw):
        term = pad_ref[8 - half + j:8 - half + j + tt, :] * w_ref[j:j + 1, :]
        acc = term if acc is None else acc + term
    return _silu(acc)


def _l2norm(x):
    return x * lax.rsqrt(jnp.sum(x * x, axis=-1, keepdims=True) + NORM_EPS)


def _rope(x, cos, sin):
    quarter = GDN_HEAD_DIM // 4
    lane = lax.broadcasted_iota(jnp.int32, x.shape, 1)
    first = (lane % (2 * quarter)) < quarter
    partner = jnp.where(first, pltpu.roll(x, GDN_HEAD_DIM - quarter, 1), pltpu.roll(x, quarter, 1))
    return x * cos + partner * sin


def _gdn_prep_kernel(*refs, tt, use_rope):
    (qp, qx, qn, kp, kx, kn, vp, vx, vn, wq_ref, wk_ref, wv_ref), rest = refs[:12], refs[12:]
    if use_rope:
        cos_ref, sin_ref, q_ref, k_ref, v_ref, pad_ref = rest
    else:
        q_ref, k_ref, v_ref, pad_ref = rest
    q = _l2norm(_conv_silu(qp, qx, qn, wq_ref, pad_ref, tt))
    if use_rope:
        q = _rope(q, cos_ref[...], sin_ref[...])
    q_ref[...] = q * (GDN_HEAD_DIM ** -0.5)
    k = _l2norm(_conv_silu(kp, kx, kn, wk_ref, pad_ref, tt))
    if use_rope:
        k = _rope(k, cos_ref[...], sin_ref[...])
    k_ref[...] = k
    v_ref[...] = _conv_silu(vp, vx, vn, wv_ref, pad_ref, tt)


def _gdn_prep_call(gqkv, conv_w, layer, rope_tabs, tt):
    b, t, w3 = gqkv.shape
    w = w3 // 3
    heads = w // GDN_HEAD_DIM
    kw = conv_w.shape[1]
    use_rope = rope_tabs is not None
    hd = GDN_HEAD_DIM
    t8, n8 = tt // 8, t // 8

    in_specs, args = [], []
    for part in range(3):
        col = lambda h, part=part: part * heads + h
        in_specs += [
            pl.BlockSpec((None, 8, hd), lambda bi, h, ti, col=col: (bi, jnp.maximum(ti * t8 - 1, 0), col(h))),
            pl.BlockSpec((None, tt, hd), lambda bi, h, ti, col=col: (bi, ti, col(h))),
            pl.BlockSpec((None, 8, hd), lambda bi, h, ti, col=col: (bi, jnp.minimum((ti + 1) * t8, n8 - 1), col(h))),
        ]
        args += [gqkv, gqkv, gqkv]
    for part in range(3):
        in_specs.append(pl.BlockSpec((None, kw, hd), lambda bi, h, ti, part=part: (layer, 0, part * heads + h)))
        args.append(conv_w)
    if use_rope:
        in_specs += [pl.BlockSpec((tt, hd), lambda bi, h, ti: (ti, 0))] * 2
        args += list(rope_tabs)
    ospec = pl.BlockSpec((None, tt, hd), lambda bi, h, ti: (bi, ti, h))
    return pl.pallas_call(
        functools.partial(_gdn_prep_kernel, tt=tt, use_rope=use_rope),
        grid=(b, heads, t // tt),
        in_specs=in_specs,
        out_specs=[ospec, ospec, ospec],
        out_shape=[jax.ShapeDtypeStruct((b, t, w), F32)] * 3,
        scratch_shapes=[pltpu.VMEM((tt + 16, hd), F32)],
        compiler_params=_params("parallel", "parallel", "parallel"),
        name="gdn_prep",
    )(*args)


def _rope_tables(t):
    pos = jnp.arange(t)
    row = (pos // GRID_W).astype(F32)
    col = (pos % GRID_W).astype(F32)
    axis_dim = GDN_HEAD_DIM // 2
    inv_freq = ROPE_BASE ** (-jnp.arange(0, axis_dim, 2, dtype=F32) / axis_dim)
    ang_r = row[:, None] * inv_freq[None, :]
    ang_c = col[:, None] * inv_freq[None, :]
    cos = jnp.concatenate([jnp.cos(ang_r), jnp.cos(ang_r), jnp.cos(ang_c), jnp.cos(ang_c)], axis=-1)
    sin = jnp.concatenate([-jnp.sin(ang_r), jnp.sin(ang_r), -jnp.sin(ang_c), jnp.sin(ang_c)], axis=-1)
    return cos, sin


def _gates_kernel(ba_ref, alog_ref, dtb_ref, o_ref, *, heads, tt):
    x = ba_ref[...]
    c = GDN_CHUNK
    lane = lax.broadcasted_iota(jnp.int32, x.shape, 1)
    row = lax.broadcasted_iota(jnp.int32, x.shape, 0) % c
    beta = jax.nn.sigmoid(x)
    g = -jnp.exp(alog_ref[...]) * _softplus(x + dtb_ref[...])
    pre, suf = g, g
    s = 1
    while s < c:
        pre = pre + jnp.where(row >= s, pltpu.roll(pre, s, 0), 0.0)
        suf = suf + jnp.where(row < c - s, pltpu.roll(suf, tt - s, 0), 0.0)
        s *= 2
    gc = jnp.where(lane < 3 * heads, pre, suf)
    o_ref[...] = jnp.where(lane < 2 * heads, beta, gc)


def _gates_call(ba, alog_row, dtb_row, layer, heads, tt):
    b, t, w = ba.shape
    return pl.pallas_call(
        functools.partial(_gates_kernel, heads=heads, tt=tt),
        grid=(b, t // tt),
        in_specs=[pl.BlockSpec((None, tt, w), lambda bi, ti: (bi, ti, 0)),
                  pl.BlockSpec((None, 1, w), lambda bi, ti: (layer, 0, 0)),
                  pl.BlockSpec((None, 1, w), lambda bi, ti: (layer, 0, 0))],
        out_specs=pl.BlockSpec((None, tt, w), lambda bi, ti: (bi, ti, 0)),
        out_shape=jax.ShapeDtypeStruct((b, t, w), F32),
        compiler_params=_params("parallel", "parallel"),
        name="gdn_gates",
    )(ba, alog_row, dtb_row)


def _gdn_chain(q, k, v, beta, gc, gcrow, s_prev, fwd):
    c, d = q.shape
    ri = lax.broadcasted_iota(jnp.int32, (c, c), 0)
    ci = lax.broadcasted_iota(jnp.int32, (c, c), 1)
    if fwd:
        lower, strict, last = ri >= ci, ri > ci, c - 1
    else:
        lower, strict, last = ri <= ci, ri < ci, 0
    decay = jnp.exp(jnp.where(lower, gc - gcrow, -jnp.inf))
    kb = k * beta
    qk = _dot_nt(jnp.concatenate([q, kb], axis=0).astype(BF16), k.astype(BF16))
    attn = qk[:c] * decay
    n = -jnp.where(strict, qk[c:] * decay, 0.0)
    inv = (ri == ci).astype(F32) + n
    p = n
    for _ in range(int(math.log2(c)) - 1):
        pb = p.astype(BF16)
        p = _dot(pb, pb)
        inv = inv + _dot(inv.astype(BF16), p.astype(BF16))
    eg = jnp.exp(gc)
    g_last = gc[last:last + 1, :]
    uw = _dot(inv.astype(BF16), jnp.concatenate([v * beta, kb * eg], axis=1).astype(BF16))
    ws = _dot(jnp.concatenate([uw[:, d:], q * eg], axis=0).astype(BF16), s_prev.astype(BF16))
    v_new = uw[:, :d] - ws[:c]
    v_new_b = v_new.astype(BF16)
    o = ws[c:] + _dot(attn.astype(BF16), v_new_b)
    kd = k * jnp.exp(g_last - gc)
    s_new = s_prev * jnp.exp(g_last) + _dot_tn(kd.astype(BF16), v_new_b)
    return o, s_new


def _gdn_scan_kernel(qf, kf, vf, gf, qb, kb, vb, gb, s0_ref, of_ref, ob_ref, sfin_ref, s_scr, *, heads):
    i = pl.program_id(1)
    n = pl.num_programs(1)
    hd = GDN_HEAD_DIM

    @pl.when(i == 0)
    def _():
        s_scr[...] = s0_ref[...]

    for d, (q_ref, k_ref, v_ref, g_ref, o_ref) in enumerate(((qf, kf, vf, gf, of_ref), (qb, kb, vb, gb, ob_ref))):
        gates = g_ref[...]
        gates_t = gates.T
        outs = []
        for h in range(heads):
            sl = slice(h * hd, (h + 1) * hd)
            lb = d * heads + h
            lg = 2 * heads + lb
            o, s_new = _gdn_chain(q_ref[:, sl], k_ref[:, sl], v_ref[:, sl], gates[:, lb:lb + 1],
                                  gates[:, lg:lg + 1], gates_t[lg:lg + 1, :], s_scr[d, h], d == 0)
            s_scr[d, h] = s_new
            outs.append(o)
        o_ref[...] = jnp.concatenate(outs, axis=1)

    @pl.when(i == n - 1)
    def _():
        sfin_ref[...] = s_scr[...]


def _gdn_scan_call(q, k, v, gates, s0):
    b, t, w = q.shape
    heads = w // GDN_HEAD_DIM
    c = GDN_CHUNK
    n = t // c
    hd = GDN_HEAD_DIM
    fspec = lambda width: pl.BlockSpec((None, c, width), lambda bi, i: (bi, i, 0))
    bspec = lambda width: pl.BlockSpec((None, c, width), lambda bi, i: (bi, n - 1 - i, 0))
    sspec = pl.BlockSpec((None, 2, heads, hd, hd), lambda bi, i: (bi, 0, 0, 0, 0))
    gw = gates.shape[-1]
    return pl.pallas_call(
        functools.partial(_gdn_scan_kernel, heads=heads),
        grid=(b, n),
        in_specs=[fspec(w), fspec(w), fspec(w), fspec(gw), bspec(w), bspec(w), bspec(w), bspec(gw), sspec],
        out_specs=[fspec(w), bspec(w), sspec],
        out_shape=[jax.ShapeDtypeStruct((b, t, w), F32), jax.ShapeDtypeStruct((b, t, w), F32),
                   jax.ShapeDtypeStruct((b, 2, heads, hd, hd), F32)],
        scratch_shapes=[pltpu.VMEM((2, heads, hd, hd), F32)],
        compiler_params=_params("parallel", "arbitrary"),
        name="gdn_scan",
    )(q, k, v, gates, q, k, v, gates, s0)


def _outproj_kernel(na_ref, of_ref, ob_ref, z_ref, x_ref, mod_ref, nw_ref, w_ref, g_ref, b_ref, o_ref, *, alpha):
    hd = GDN_HEAD_DIM
    o = of_ref[...] + ob_ref[...]
    z = z_ref[...]
    parts = []
    for h in range(o.shape[-1] // hd):
        oh = o[:, h * hd:(h + 1) * hd]
        oh = oh * lax.rsqrt(jnp.mean(oh * oh, axis=-1, keepdims=True) + NORM_EPS) * nw_ref[...]
        parts.append(oh * _silu(z[:, h * hd:(h + 1) * hd]))
    gdn = jnp.concatenate(parts, axis=1).astype(BF16)
    na_w = na_ref.shape[-1]
    y = _dot(na_ref[...], w_ref[0:na_w, :]) + _dot(gdn, w_ref[na_w:, :])
    o_ref[...] = _layer_norm(alpha * x_ref[...] + mod_ref[2:3, :] * y, g_ref[...], b_ref[...])


def _outproj_call(na_out, o_f, o_b, z, x, mods, nw, w_out_b, ln_g, ln_b, layer, mod_row, tm, alpha):
    b, t, d = x.shape
    na_w, gw = na_out.shape[-1], o_f.shape[-1]
    mod_idx = (lambda bi, ti: (layer, bi, 0, 0)) if mod_row is None else (lambda bi, ti: (layer, mod_row, 0, 0))
    tok = lambda width: pl.BlockSpec((None, tm, width), lambda bi, ti: (bi, ti, 0))
    lay = lambda *shape: pl.BlockSpec((None,) + shape, lambda bi, ti: (layer,) + (0,) * len(shape))
    return pl.pallas_call(
        functools.partial(_outproj_kernel, alpha=alpha),
        grid=(b, t // tm),
        in_specs=[tok(na_w), tok(gw), tok(gw), tok(gw), tok(d),
                  pl.BlockSpec((None, None, N_MOD, d), mod_idx),
                  lay(1, GDN_HEAD_DIM), lay(na_w + gw, d), lay(1, d), lay(1, d)],
        out_specs=tok(d),
        out_shape=jax.ShapeDtypeStruct((b, t, d), F32),
        compiler_params=_params("parallel", "parallel"),
        name="out_proj_ln",
    )(na_out, o_f, o_b, z, x, mods, nw, w_out_b, ln_g, ln_b)


def _mlp_kernel(x_ref, mod_ref, w1_ref, w2_ref, g_ref, b_ref, o_ref, *, alpha, n_split):
    x = x_ref[...]
    hb = (x * (1.0 + mod_ref[4:5, :]) + mod_ref[3:4, :]).astype(BF16)
    ck = w1_ref.shape[1] // n_split
    acc = None
    for j in range(n_split):
        a = jnp.maximum(_dot(hb, w1_ref[:, j * ck:(j + 1) * ck]), 0.0)
        part = _dot((a * a).astype(BF16), w2_ref[j * ck:(j + 1) * ck, :])
        acc = part if acc is None else acc + part
    o_ref[...] = _layer_norm(alpha * x + mod_ref[5:6, :] * acc, g_ref[...], b_ref[...])


def _mlp_call(x, mods, w1_b, w2_b, ln_g, ln_b, layer, mod_row, tm, alpha):
    b, t, d = x.shape
    dff = w1_b.shape[-1]
    mod_idx = (lambda bi, ti: (layer, bi, 0, 0)) if mod_row is None else (lambda bi, ti: (layer, mod_row, 0, 0))
    tok = pl.BlockSpec((None, tm, d), lambda bi, ti: (bi, ti, 0))
    lay = lambda *shape: pl.BlockSpec((None,) + shape, lambda bi, ti: (layer,) + (0,) * len(shape))
    return pl.pallas_call(
        functools.partial(_mlp_kernel, alpha=alpha, n_split=4),
        grid=(b, t // tm),
        in_specs=[tok, pl.BlockSpec((None, None, N_MOD, d), mod_idx),
                  lay(d, dff), lay(dff, d), lay(1, d), lay(1, d)],
        out_specs=tok,
        out_shape=jax.ShapeDtypeStruct((b, t, d), F32),
        compiler_params=_params("parallel", "parallel"),
        name="mlp_ln",
    )(x, mods, w1_b, w2_b, ln_g, ln_b)


def _token_tile(t, target):
    return target if t % target == 0 else t


def kernel(x, c, ctx, c_ctx, w_ada, b_ada, w_in, conv_w, a_log, dt_bias, gdn_norm_w, rpb, w_out, ln1_g, ln1_b,
           w_mlp1, w_mlp2, ln2_g, ln2_b):
    depth, d, _ = w_ada.shape
    b, t, _ = x.shape
    l = ctx.shape[1]
    gdn_heads = a_log.shape[-1]
    alpha = (2 * depth) ** 0.25
    assert b + 1 <= MOD_ROWS and t % GRID_W == 0 and t % GDN_CHUNK == 0 and l % GDN_CHUNK == 0

    cc = jnp.zeros((MOD_ROWS, d), F32).at[:b].set(c).at[b].set(c_ctx)
    mods = _ada_call(cc, w_ada, b_ada).reshape(depth, MOD_ROWS, N_MOD, d)
    w_in_b = jnp.pad(w_in, ((0, 0), (0, 0), (0, BA_PAD - 4 * gdn_heads))).astype(BF16)
    w_out_b = w_out.astype(BF16)
    w1_b = w_mlp1.astype(BF16)
    w2_b = w_mlp2.astype(BF16)
    gate_pad = ((0, 0), (0, 0), (2 * gdn_heads, BA_PAD - 4 * gdn_heads))
    alog_row = jnp.pad(a_log.reshape(depth, 1, 2 * gdn_heads), gate_pad)
    dtb_row = jnp.pad(dt_bias.reshape(depth, 1, 2 * gdn_heads), gate_pad)
    nw = gdn_norm_w.reshape(depth, 1, GDN_HEAD_DIM)
    ln1g, ln1b = ln1_g.reshape(depth, 1, d), ln1_b.reshape(depth, 1, d)
    ln2g, ln2b = ln2_g.reshape(depth, 1, d), ln2_b.reshape(depth, 1, d)
    rope_tabs = _rope_tables(t)
    win_r = min(NA_WIN_R, t // GRID_W)
    zeros_state = jnp.zeros((b, 2, gdn_heads, GDN_HEAD_DIM, GDN_HEAD_DIM), F32)

    tm_lat, tm_ctx = _token_tile(t, 512), _token_tile(l, 256)
    x_lat, x_ctx = x, ctx
    for layer in range(depth):
        na_l, g_l, z_l, ba_l = _inproj_call(x_lat, mods, w_in_b, layer, None, tm_lat)
        na_c, g_c, z_c, ba_c = _inproj_call(x_ctx, mods, w_in_b, layer, b, tm_ctx)

        na_out_l = _na_call(na_l, na_c, _na_bias_table(rpb[layer], win_r))

        qc, kc, vc = _gdn_prep_call(g_c, conv_w, layer, None, tm_ctx)
        gates_c = _gates_call(ba_c, alog_row, dtb_row, layer, gdn_heads, tm_ctx)
        of_c, ob_c, s_ctx = _gdn_scan_call(qc, kc, vc, gates_c, zeros_state)
        ql, kl, vl = _gdn_prep_call(g_l, conv_w, layer, rope_tabs, tm_lat)
        gates_l = _gates_call(ba_l, alog_row, dtb_row, layer, gdn_heads, tm_lat)
        of_l, ob_l, _ = _gdn_scan_call(ql, kl, vl, gates_l, s_ctx)

        x_lat = _outproj_call(na_out_l, of_l, ob_l, z_l, x_lat, mods, nw, w_out_b, ln1g, ln1b, layer, None,
                              tm_lat, alpha)
        x_lat = _mlp_call(x_lat, mods, w1_b, w2_b, ln2g, ln2b, layer, None, tm_lat, alpha)
        if layer < depth - 1:
            na_out_c = _ctx_attn_call(na_c)
            x_ctx = _outproj_call(na_out_c, of_c, ob_c, z_c, x_ctx, mods, nw, w_out_b, ln1g, ln1b, layer, b,
                                  tm_ctx, alpha)
            x_ctx = _mlp_call(x_ctx, mods, w1_b, w2_b, ln2g, ln2b, layer, b, tm_ctx, alpha)
    return x_lat
```

```python
import functools
import math

import jax
import jax.numpy as jnp
from jax import lax
from jax.experimental import pallas as pl
from jax.experimental.pallas import tpu as pltpu

GRID_W = 64
NA_HEAD_DIM = 64
NA_WIN_R = 8
NA_WIN_C = 16
GDN_HEAD_DIM = 128
GDN_CHUNK = 64
ROPE_BASE = 10000.0
N_MOD = 6
LN_EPS = 1e-5
NORM_EPS = 1e-6
MOD_ROWS = 16
BA_PAD = 128
VMEM_LIMIT = 56 * 1024 * 1024

BF16 = jnp.bfloat16
F32 = jnp.float32


def _params(*sem):
    return pltpu.CompilerParams(dimension_semantics=sem, vmem_limit_bytes=VMEM_LIMIT)


def _dot(a, b):
    return jnp.dot(a, b, preferred_element_type=F32)


def _dot_nt(a, b):
    return lax.dot_general(a, b, (((1,), (1,)), ((), ())), preferred_element_type=F32)


def _dot_tn(a, b):
    return lax.dot_general(a, b, (((0,), (0,)), ((), ())), preferred_element_type=F32)


def _silu(x):
    return x * jax.nn.sigmoid(x)


def _softplus(x):
    return jnp.maximum(x, 0.0) + jnp.log1p(jnp.exp(-jnp.abs(x)))


def _layer_norm(x, g, b):
    mu = jnp.mean(x, axis=-1, keepdims=True)
    xc = x - mu
    var = jnp.mean(xc * xc, axis=-1, keepdims=True)
    return xc * lax.rsqrt(var + LN_EPS) * g + b


def _ada_kernel(c_ref, w_ref, b_ref, o_ref):
    o_ref[...] = _dot(_silu(c_ref[...]), w_ref[...]) + b_ref[...]


def _ada_call(cc, w_ada, b_ada):
    depth, d, n = w_ada.shape
    tn = 1536
    return pl.pallas_call(
        _ada_kernel,
        grid=(depth, n // tn),
        in_specs=[pl.BlockSpec((MOD_ROWS, d), lambda l, j: (0, 0)),
                  pl.BlockSpec((None, d, tn), lambda l, j: (l, 0, j)),
                  pl.BlockSpec((None, 1, tn), lambda l, j: (l, 0, j))],
        out_specs=pl.BlockSpec((None, MOD_ROWS, tn), lambda l, j: (l, 0, j)),
        out_shape=jax.ShapeDtypeStruct((depth, MOD_ROWS, n), F32),
        compiler_params=_params("parallel", "parallel"),
        name="ada_mod",
    )(cc, w_ada, b_ada.reshape(depth, 1, n))


def _inproj_kernel(x_ref, mod_ref, w_ref, na_ref, g_ref, z_ref, ba_ref, *, na3, g3, zw):
    h = x_ref[...] * (1.0 + mod_ref[1:2, :]) + mod_ref[0:1, :]
    hb = h.astype(BF16)
    na_ref[...] = _dot(hb, w_ref[:, 0:na3]).astype(na_ref.dtype)
    g_ref[...] = _dot(hb, w_ref[:, na3:na3 + g3])
    z_ref[...] = _dot(hb, w_ref[:, na3 + g3:na3 + g3 + zw])
    ba_ref[...] = _dot(hb, w_ref[:, na3 + g3 + zw:])


def _inproj_call(x, mods, w_in_b, layer, mod_row, tm):
    b, t, d = x.shape
    na3, g3, zw = 3 * (d // 2), 3 * (d // 2), d // 2
    ncols = w_in_b.shape[-1]
    mod_idx = (lambda bi, ti: (layer, bi, 0, 0)) if mod_row is None else (lambda bi, ti: (layer, mod_row, 0, 0))
    return pl.pallas_call(
        functools.partial(_inproj_kernel, na3=na3, g3=g3, zw=zw),
        grid=(b, t // tm),
        in_specs=[pl.BlockSpec((None, tm, d), lambda bi, ti: (bi, ti, 0)),
                  pl.BlockSpec((None, None, N_MOD, d), mod_idx),
                  pl.BlockSpec((None, d, ncols), lambda bi, ti: (layer, 0, 0))],
        out_specs=[pl.BlockSpec((None, tm, na3), lambda bi, ti: (bi, ti, 0)),
                   pl.BlockSpec((None, tm, g3), lambda bi, ti: (bi, ti, 0)),
                   pl.BlockSpec((None, tm, zw), lambda bi, ti: (bi, ti, 0)),
                   pl.BlockSpec((None, tm, BA_PAD), lambda bi, ti: (bi, ti, 0))],
        out_shape=[jax.ShapeDtypeStruct((b, t, na3), BF16),
                   jax.ShapeDtypeStruct((b, t, g3), F32),
                   jax.ShapeDtypeStruct((b, t, zw), F32),
                   jax.ShapeDtypeStruct((b, t, BA_PAD), F32)],
        compiler_params=_params("parallel", "parallel"),
        name="in_proj",
    )(x, mods, w_in_b)


def _softmax_pv(s_list, v_list):
    m = None
    for s in s_list:
        sm = jnp.max(s, axis=-1, keepdims=True)
        m = sm if m is None else jnp.maximum(m, sm)
    den = None
    acc = None
    for s, v in zip(s_list, v_list):
        p = jnp.exp(s - m)
        ps = jnp.sum(p, axis=-1, keepdims=True)
        den = ps if den is None else den + ps
        pv = _dot(p.astype(BF16), v)
        acc = pv if acc is None else acc + pv
    return acc / den


def _na_kernel(q_ref, k_ref, v_ref, kc_ref, vc_ref, bias_ref, o_ref, *, heads, rows, win_r):
    r = pl.program_id(1)
    n_band = win_r * GRID_W
    row_start = jnp.clip(r - win_r // 2, 0, rows - win_r)
    koff = pl.multiple_of(row_start * GRID_W, GRID_W)
    scale = NA_HEAD_DIM ** -0.5
    pairs = range(heads // 2)
    gq = q_ref.shape[0]
    lane = lax.broadcasted_iota(jnp.int32, (gq, 2 * NA_HEAD_DIM), 1)
    qs, s_lat, s_ctx = [], [], []
    for pr in pairs:
        sl = slice(2 * pr * NA_HEAD_DIM, 2 * (pr + 1) * NA_HEAD_DIM)
        qp = q_ref[:, sl]
        zero = jnp.zeros_like(qp)
        qs.append(jnp.concatenate([jnp.where(lane < NA_HEAD_DIM, qp, zero),
                                   jnp.where(lane >= NA_HEAD_DIM, qp, zero)], axis=0))
    for pr in pairs:
        sl = slice(2 * pr * NA_HEAD_DIM, 2 * (pr + 1) * NA_HEAD_DIM)
        bias = bias_ref[2 * pr:2 * pr + 2].reshape(2 * gq, n_band)
        s_lat.append(_dot_nt(qs[pr], k_ref[pl.ds(koff, n_band), sl]) * scale + bias)
        s_ctx.append(_dot_nt(qs[pr], kc_ref[:, sl]) * scale)
    m = [jnp.maximum(jnp.max(s_lat[pr], axis=-1, keepdims=True), jnp.max(s_ctx[pr], axis=-1, keepdims=True))
         for pr in pairs]
    p_lat = [jnp.exp(s_lat[pr] - m[pr]) for pr in pairs]
    p_ctx = [jnp.exp(s_ctx[pr] - m[pr]) for pr in pairs]
    den = [jnp.sum(p_lat[pr], axis=-1, keepdims=True) + jnp.sum(p_ctx[pr], axis=-1, keepdims=True) for pr in pairs]
    outs = []
    for pr in pairs:
        sl = slice(2 * pr * NA_HEAD_DIM, 2 * (pr + 1) * NA_HEAD_DIM)
        pv = (_dot(p_lat[pr].astype(BF16), v_ref[pl.ds(koff, n_band), sl])
              + _dot(p_ctx[pr].astype(BF16), vc_ref[:, sl])) / den[pr]
        outs.append(jnp.where(lane < NA_HEAD_DIM, pv[:gq], pv[gq:]))
    o_ref[...] = jnp.concatenate(outs, axis=-1).astype(o_ref.dtype)


def _na_bias_table(rpb, win_r):
    cols = jnp.arange(GRID_W)
    col_start = jnp.clip(cols - NA_WIN_C // 2, 0, GRID_W - NA_WIN_C)
    col_in = (cols[None, :] >= col_start[:, None]) & (cols[None, :] < col_start[:, None] + NA_WIN_C)
    dc_idx = jnp.clip(cols[None, :] - cols[:, None], 1 - NA_WIN_C, NA_WIN_C - 1) + NA_WIN_C - 1
    rpb_cols = rpb[:, :, dc_idx].astype(F32)
    rpb_cols = jnp.where(col_in[None, None], rpb_cols, -jnp.inf)
    h = rpb.shape[0]
    variants = []
    for dv in range(win_r):
        dr0 = dv + NA_WIN_R - win_r
        band = rpb_cols[:, dr0:dr0 + win_r]
        variants.append(jnp.transpose(band, (0, 2, 1, 3)).reshape(h, GRID_W, win_r * GRID_W))
    return jnp.stack(variants)


def _na_call(na_lat, na_ctx, bias_tab):
    b, t, w3 = na_lat.shape
    w = w3 // 3
    heads = w // NA_HEAD_DIM
    l = na_ctx.shape[1]
    rows = t // GRID_W
    win_r = min(NA_WIN_R, rows)
    n_band = win_r * GRID_W

    def bias_idx(bi, r):
        row_start = jnp.clip(r - win_r // 2, 0, rows - win_r)
        return (row_start - r + win_r - 1, 0, 0, 0)

    return pl.pallas_call(
        functools.partial(_na_kernel, heads=heads, rows=rows, win_r=win_r),
        grid=(b, rows),
        in_specs=[pl.BlockSpec((None, GRID_W, w), lambda bi, r: (bi, r, 0)),
                  pl.BlockSpec((None, t, w), lambda bi, r: (bi, 0, 1)),
                  pl.BlockSpec((None, t, w), lambda bi, r: (bi, 0, 2)),
                  pl.BlockSpec((None, l, w), lambda bi, r: (bi, 0, 1)),
                  pl.BlockSpec((None, l, w), lambda bi, r: (bi, 0, 2)),
                  pl.BlockSpec((None, heads, GRID_W, n_band), bias_idx)],
        out_specs=pl.BlockSpec((None, GRID_W, w), lambda bi, r: (bi, r, 0)),
        out_shape=jax.ShapeDtypeStruct((b, t, w), BF16),
        compiler_params=_params("parallel", "arbitrary"),
        name="na_attn",
    )(na_lat, na_lat, na_lat, na_ctx, na_ctx, bias_tab)


def _ctx_attn_kernel(q_ref, k_ref, v_ref, o_ref, *, heads):
    scale = NA_HEAD_DIM ** -0.5
    outs = []
    for h in range(heads):
        sl = slice(h * NA_HEAD_DIM, (h + 1) * NA_HEAD_DIM)
        s = _dot_nt(q_ref[:, sl], k_ref[:, sl]) * scale
        outs.append(_softmax_pv([s], [v_ref[:, sl]]))
    o_ref[...] = jnp.concatenate(outs, axis=-1).astype(o_ref.dtype)


def _ctx_attn_call(na_ctx):
    b, l, w3 = na_ctx.shape
    w = w3 // 3
    return pl.pallas_call(
        functools.partial(_ctx_attn_kernel, heads=w // NA_HEAD_DIM),
        grid=(b,),
        in_specs=[pl.BlockSpec((None, l, w), lambda bi: (bi, 0, 0)),
                  pl.BlockSpec((None, l, w), lambda bi: (bi, 0, 1)),
                  pl.BlockSpec((None, l, w), lambda bi: (bi, 0, 2))],
        out_specs=pl.BlockSpec((None, l, w), lambda bi: (bi, 0, 0)),
        out_shape=jax.ShapeDtypeStruct((b, l, w), BF16),
        compiler_params=_params("parallel"),
        name="ctx_attn",
    )(na_ctx, na_ctx, na_ctx)


def _conv_silu(prev_ref, x_ref, next_ref, w_ref, pad_ref, tt):
    ti = pl.program_id(2)
    nt = pl.num_programs(2)
    kw = w_ref.shape[0]
    half = kw // 2
    pad_ref[0:8, :] = jnp.where(ti > 0, prev_ref[...], 0.0)
    pad_ref[8 + tt:16 + tt, :] = jnp.where(ti < nt - 1, next_ref[...], 0.0)
    pad_ref[8:8 + tt, :] = x_ref[...]
    acc = None
    for j in range(kw):
        term = pad_ref[8 - half + j:8 - half + j + tt, :] * w_ref[j:j + 1, :]
        acc = term if acc is None else acc + term
    return _silu(acc)


def _l2norm(x):
    return x * lax.rsqrt(jnp.sum(x * x, axis=-1, keepdims=True) + NORM_EPS)


def _rope(x, cos, sin):
    quarter = GDN_HEAD_DIM // 4
    lane = lax.broadcasted_iota(jnp.int32, x.shape, 1)
    first = (lane % (2 * quarter)) < quarter
    partner = jnp.where(first, pltpu.roll(x, GDN_HEAD_DIM - quarter, 1), pltpu.roll(x, quarter, 1))
    return x * cos + partner * sin


def _gdn_prep_kernel(*refs, tt, use_rope):
    (qp, qx, qn, kp, kx, kn, vp, vx, vn, wq_ref, wk_ref, wv_ref), rest = refs[:12], refs[12:]
    if use_rope:
        cos_ref, sin_ref, q_ref, k_ref, v_ref, pad_ref = rest
    else:
        q_ref, k_ref, v_ref, pad_ref = rest
    q = _l2norm(_conv_silu(qp, qx, qn, wq_ref, pad_ref, tt))
    if use_rope:
        q = _rope(q, cos_ref[...], sin_ref[...])
    q_ref[...] = q * (GDN_HEAD_DIM ** -0.5)
    k = _l2norm(_conv_silu(kp, kx, kn, wk_ref, pad_ref, tt))
    if use_rope:
        k = _rope(k, cos_ref[...], sin_ref[...])
    k_ref[...] = k
    v_ref[...] = _conv_silu(vp, vx, vn, wv_ref, pad_ref, tt)


def _gdn_prep_call(gqkv, conv_w, layer, rope_tabs, tt):
    b, t, w3 = gqkv.shape
    w = w3 // 3
    heads = w // GDN_HEAD_DIM
    kw = conv_w.shape[1]
    use_rope = rope_tabs is not None
    hd = GDN_HEAD_DIM
    t8, n8 = tt // 8, t // 8

    in_specs, args = [], []
    for part in range(3):
        col = lambda h, part=part: part * heads + h
        in_specs += [
            pl.BlockSpec((None, 8, hd), lambda bi, h, ti, col=col: (bi, jnp.maximum(ti * t8 - 1, 0), col(h))),
            pl.BlockSpec((None, tt, hd), lambda bi, h, ti, col=col: (bi, ti, col(h))),
            pl.BlockSpec((None, 8, hd), lambda bi, h, ti, col=col: (bi, jnp.minimum((ti + 1) * t8, n8 - 1), col(h))),
        ]
        args += [gqkv, gqkv, gqkv]
    for part in range(3):
        in_specs.append(pl.BlockSpec((None, kw, hd), lambda bi, h, ti, part=part: (layer, 0, part * heads + h)))
        args.append(conv_w)
    if use_rope:
        in_specs += [pl.BlockSpec((tt, hd), lambda bi, h, ti: (ti, 0))] * 2
        args += list(rope_tabs)
    ospec = pl.BlockSpec((None, tt, hd), lambda bi, h, ti: (bi, ti, h))
    return pl.pallas_call(
        functools.partial(_gdn_prep_kernel, tt=tt, use_rope=use_rope),
        grid=(b, heads, t // tt),
        in_specs=in_specs,
        out_specs=[ospec, ospec, ospec],
        out_shape=[jax.ShapeDtypeStruct((b, t, w), F32)] * 3,
        scratch_shapes=[pltpu.VMEM((tt + 16, hd), F32)],
        compiler_params=_params("parallel", "parallel", "parallel"),
        name="gdn_prep",
    )(*args)


def _rope_tables(t):
    pos = jnp.arange(t)
    row = (pos // GRID_W).astype(F32)
    col = (pos % GRID_W).astype(F32)
    axis_dim = GDN_HEAD_DIM // 2
    inv_freq = ROPE_BASE ** (-jnp.arange(0, axis_dim, 2, dtype=F32) / axis_dim)
    ang_r = row[:, None] * inv_freq[None, :]
    ang_c = col[:, None] * inv_freq[None, :]
    cos = jnp.concatenate([jnp.cos(ang_r), jnp.cos(ang_r), jnp.cos(ang_c), jnp.cos(ang_c)], axis=-1)
    sin = jnp.concatenate([-jnp.sin(ang_r), jnp.sin(ang_r), -jnp.sin(ang_c), jnp.sin(ang_c)], axis=-1)
    return cos, sin


def _gates_kernel(ba_ref, alog_ref, dtb_ref, o_ref, *, heads, tt):
    x = ba_ref[...]
    c = GDN_CHUNK
    lane = lax.broadcasted_iota(jnp.int32, x.shape, 1)
    row = lax.broadcasted_iota(jnp.int32, x.shape, 0) % c
    beta = jax.nn.sigmoid(x)
    g = -jnp.exp(alog_ref[...]) * _softplus(x + dtb_ref[...])
    pre, suf = g, g
    s = 1
    while s < c:
        pre = pre + jnp.where(row >= s, pltpu.roll(pre, s, 0), 0.0)
        suf = suf + jnp.where(row < c - s, pltpu.roll(suf, tt - s, 0), 0.0)
        s *= 2
    gc = jnp.where(lane < 3 * heads, pre, suf)
    o_ref[...] = jnp.where(lane < 2 * heads, beta, gc)


def _gates_call(ba, alog_row, dtb_row, layer, heads, tt):
    b, t, w = ba.shape
    return pl.pallas_call(
        functools.partial(_gates_kernel, heads=heads, tt=tt),
        grid=(b, t // tt),
        in_specs=[pl.BlockSpec((None, tt, w), lambda bi, ti: (bi, ti, 0)),
                  pl.BlockSpec((None, 1, w), lambda bi, ti: (layer, 0, 0)),
                  pl.BlockSpec((None, 1, w), lambda bi, ti: (layer, 0, 0))],
        out_specs=pl.BlockSpec((None, tt, w), lambda bi, ti: (bi, ti, 0)),
        out_shape=jax.ShapeDtypeStruct((b, t, w), F32),
        compiler_params=_params("parallel", "parallel"),
        name="gdn_gates",
    )(ba, alog_row, dtb_row)


def _gdn_chains(chains):
    c, d = chains[0][0].shape
    ri = lax.broadcasted_iota(jnp.int32, (c, c), 0)
    ci = lax.broadcasted_iota(jnp.int32, (c, c), 1)
    eye = (ri == ci).astype(F32)
    tri = {True: (ri >= ci, ri > ci, c - 1), False: (ri <= ci, ri < ci, 0)}
    nc = range(len(chains))

    kb = [k * beta for (_, k, _, beta, _, _, _, _) in chains]
    qk = [_dot_nt(jnp.concatenate([ch[0], kb[j]], axis=0).astype(BF16), ch[1].astype(BF16))
          for j, ch in enumerate(chains)]
    decay = [jnp.exp(jnp.where(tri[fwd][0], gc - gcrow, -jnp.inf)) for (_, _, _, _, gc, gcrow, _, fwd) in chains]
    attn = [(qk[j][:c] * decay[j]).astype(BF16) for j in nc]
    p = [-jnp.where(tri[chains[j][7]][1], qk[j][c:] * decay[j], 0.0) for j in nc]
    inv = [eye + p[j] for j in nc]
    for _ in range(int(math.log2(c)) - 1):
        pb = [p[j].astype(BF16) for j in nc]
        p = [_dot(pb[j], pb[j]) for j in nc]
        inv = [inv[j] + _dot(inv[j].astype(BF16), p[j].astype(BF16)) for j in nc]
    eg = [jnp.exp(ch[4]) for ch in chains]
    g_last = [ch[4][tri[ch[7]][2]:tri[ch[7]][2] + 1, :] for ch in chains]
    uw = [_dot(inv[j].astype(BF16), jnp.concatenate([ch[2] * ch[3], kb[j] * eg[j]], axis=1).astype(BF16))
          for j, ch in enumerate(chains)]
    ws = [_dot(jnp.concatenate([uw[j][:, d:], ch[0] * eg[j]], axis=0).astype(BF16), ch[6].astype(BF16))
          for j, ch in enumerate(chains)]
    v_new = [(uw[j][:, :d] - ws[j][:c]).astype(BF16) for j in nc]
    kd = [(ch[1] * jnp.exp(g_last[j] - ch[4])).astype(BF16) for j, ch in enumerate(chains)]
    o = [ws[j][c:] + _dot(attn[j], v_new[j]) for j in nc]
    s_new = [ch[6] * jnp.exp(g_last[j]) + _dot_tn(kd[j], v_new[j]) for j, ch in enumerate(chains)]
    return o, s_new


def _gdn_scan_kernel(qf, kf, vf, gf, qb, kb, vb, gb, s0_ref, of_ref, ob_ref, sfin_ref, s_scr, *, heads):
    i = pl.program_id(1)
    n = pl.num_programs(1)
    hd = GDN_HEAD_DIM

    @pl.when(i == 0)
    def _():
        s_scr[...] = s0_ref[...]

    chains = []
    for d, (q_ref, k_ref, v_ref, g_ref) in enumerate(((qf, kf, vf, gf), (qb, kb, vb, gb))):
        gates = g_ref[...]
        gates_t = gates.T
        for h in range(heads):
            sl = slice(h * hd, (h + 1) * hd)
            lb = d * heads + h
            lg = 2 * heads + lb
            chains.append((q_ref[:, sl], k_ref[:, sl], v_ref[:, sl], gates[:, lb:lb + 1], gates[:, lg:lg + 1],
                           gates_t[lg:lg + 1, :], s_scr[d, h], d == 0))
    o, s_new = _gdn_chains(chains)
    for d, o_ref in enumerate((of_ref, ob_ref)):
        o_ref[...] = jnp.concatenate(o[d * heads:(d + 1) * heads], axis=1)
        for h in range(heads):
            s_scr[d, h] = s_new[d * heads + h]

    @pl.when(i == n - 1)
    def _():
        sfin_ref[...] = s_scr[...]


def _gdn_scan_call(q, k, v, gates, s0):
    b, t, w = q.shape
    heads = w // GDN_HEAD_DIM
    c = GDN_CHUNK
    n = t // c
    hd = GDN_HEAD_DIM
    fspec = lambda width: pl.BlockSpec((None, c, width), lambda bi, i: (bi, i, 0))
    bspec = lambda width: pl.BlockSpec((None, c, width), lambda bi, i: (bi, n - 1 - i, 0))
    sspec = pl.BlockSpec((None, 2, heads, hd, hd), lambda bi, i: (bi, 0, 0, 0, 0))
    gw = gates.shape[-1]
    return pl.pallas_call(
        functools.partial(_gdn_scan_kernel, heads=heads),
        grid=(b, n),
        in_specs=[fspec(w), fspec(w), fspec(w), fspec(gw), bspec(w), bspec(w), bspec(w), bspec(gw), sspec],
        out_specs=[fspec(w), bspec(w), sspec],
        out_shape=[jax.ShapeDtypeStruct((b, t, w), F32), jax.ShapeDtypeStruct((b, t, w), F32),
                   jax.ShapeDtypeStruct((b, 2, heads, hd, hd), F32)],
        scratch_shapes=[pltpu.VMEM((2, heads, hd, hd), F32)],
        compiler_params=_params("parallel", "arbitrary"),
        name="gdn_scan",
    )(q, k, v, gates, q, k, v, gates, s0)


def _outproj_kernel(na_ref, of_ref, ob_ref, z_ref, x_ref, mod_ref, nw_ref, w_ref, g_ref, b_ref, o_ref, *, alpha):
    hd = GDN_HEAD_DIM
    o = of_ref[...] + ob_ref[...]
    z = z_ref[...]
    parts = []
    for h in range(o.shape[-1] // hd):
        oh = o[:, h * hd:(h + 1) * hd]
        oh = oh * lax.rsqrt(jnp.mean(oh * oh, axis=-1, keepdims=True) + NORM_EPS) * nw_ref[...]
        parts.append(oh * _silu(z[:, h * hd:(h + 1) * hd]))
    gdn = jnp.concatenate(parts, axis=1).astype(BF16)
    na_w = na_ref.shape[-1]
    y = _dot(na_ref[...], w_ref[0:na_w, :]) + _dot(gdn, w_ref[na_w:, :])
    o_ref[...] = _layer_norm(alpha * x_ref[...] + mod_ref[2:3, :] * y, g_ref[...], b_ref[...])


def _outproj_call(na_out, o_f, o_b, z, x, mods, nw, w_out_b, ln_g, ln_b, layer, mod_row, tm, alpha):
    b, t, d = x.shape
    na_w, gw = na_out.shape[-1], o_f.shape[-1]
    mod_idx = (lambda bi, ti: (layer, bi, 0, 0)) if mod_row is None else (lambda bi, ti: (layer, mod_row, 0, 0))
    tok = lambda width: pl.BlockSpec((None, tm, width), lambda bi, ti: (bi, ti, 0))
    lay = lambda *shape: pl.BlockSpec((None,) + shape, lambda bi, ti: (layer,) + (0,) * len(shape))
    return pl.pallas_call(
        functools.partial(_outproj_kernel, alpha=alpha),
        grid=(b, t // tm),
        in_specs=[tok(na_w), tok(gw), tok(gw), tok(gw), tok(d),
                  pl.BlockSpec((None, None, N_MOD, d), mod_idx),
                  lay(1, GDN_HEAD_DIM), lay(na_w + gw, d), lay(1, d), lay(1, d)],
        out_specs=tok(d),
        out_shape=jax.ShapeDtypeStruct((b, t, d), F32),
        compiler_params=_params("parallel", "parallel"),
        name="out_proj_ln",
    )(na_out, o_f, o_b, z, x, mods, nw, w_out_b, ln_g, ln_b)


def _mlp_kernel(x_ref, mod_ref, w1_ref, w2_ref, g_ref, b_ref, o_ref, *, alpha, n_split):
    x = x_ref[...]
    hb = (x * (1.0 + mod_ref[4:5, :]) + mod_ref[3:4, :]).astype(BF16)
    ck = w1_ref.shape[1] // n_split
    acc = None
    for j in range(n_split):
        a = jnp.maximum(_dot(hb, w1_ref[:, j * ck:(j + 1) * ck]), 0.0)
        part = _dot((a * a).astype(BF16), w2_ref[j * ck:(j + 1) * ck, :])
        acc = part if acc is None else acc + part
    o_ref[...] = _layer_norm(alpha * x + mod_ref[5:6, :] * acc, g_ref[...], b_ref[...])


def _mlp_call(x, mods, w1_b, w2_b, ln_g, ln_b, layer, mod_row, tm, alpha):
    b, t, d = x.shape
    dff = w1_b.shape[-1]
    mod_idx = (lambda bi, ti: (layer, bi, 0, 0)) if mod_row is None else (lambda bi, ti: (layer, mod_row, 0, 0))
    tok = pl.BlockSpec((None, tm, d), lambda bi, ti: (bi, ti, 0))
    lay = lambda *shape: pl.BlockSpec((None,) + shape, lambda bi, ti: (layer,) + (0,) * len(shape))
    return pl.pallas_call(
        functools.partial(_mlp_kernel, alpha=alpha, n_split=4),
        grid=(b, t // tm),
        in_specs=[tok, pl.BlockSpec((None, None, N_MOD, d), mod_idx),
                  lay(d, dff), lay(dff, d), lay(1, d), lay(1, d)],
        out_specs=tok,
        out_shape=jax.ShapeDtypeStruct((b, t, d), F32),
        compiler_params=_params("parallel", "parallel"),
        name="mlp_ln",
    )(x, mods, w1_b, w2_b, ln_g, ln_b)


def _token_tile(t, target):
    return target if t % target == 0 else t


def kernel(x, c, ctx, c_ctx, w_ada, b_ada, w_in, conv_w, a_log, dt_bias, gdn_norm_w, rpb, w_out, ln1_g, ln1_b,
           w_mlp1, w_mlp2, ln2_g, ln2_b):
    depth, d, _ = w_ada.shape
    b, t, _ = x.shape
    l = ctx.shape[1]
    gdn_heads = a_log.shape[-1]
    alpha = (2 * depth) ** 0.25
    assert b + 1 <= MOD_ROWS and t % GRID_W == 0 and t % GDN_CHUNK == 0 and l % GDN_CHUNK == 0

    cc = jnp.zeros((MOD_ROWS, d), F32).at[:b].set(c).at[b].set(c_ctx)
    mods = _ada_call(cc, w_ada, b_ada).reshape(depth, MOD_ROWS, N_MOD, d)
    w_in_b = jnp.pad(w_in, ((0, 0), (0, 0), (0, BA_PAD - 4 * gdn_heads))).astype(BF16)
    w_out_b = w_out.astype(BF16)
    w1_b = w_mlp1.astype(BF16)
    w2_b = w_mlp2.astype(BF16)
    gate_pad = ((0, 0), (0, 0), (2 * gdn_heads, BA_PAD - 4 * gdn_heads))
    alog_row = jnp.pad(a_log.reshape(depth, 1, 2 * gdn_heads), gate_pad)
    dtb_row = jnp.pad(dt_bias.reshape(depth, 1, 2 * gdn_heads), gate_pad)
    nw = gdn_norm_w.reshape(depth, 1, GDN_HEAD_DIM)
    ln1g, ln1b = ln1_g.reshape(depth, 1, d), ln1_b.reshape(depth, 1, d)
    ln2g, ln2b = ln2_g.reshape(depth, 1, d), ln2_b.reshape(depth, 1, d)
    rope_tabs = _rope_tables(t)
    win_r = min(NA_WIN_R, t // GRID_W)
    zeros_state = jnp.zeros((b, 2, gdn_heads, GDN_HEAD_DIM, GDN_HEAD_DIM), F32)

    tm_lat, tm_ctx = _token_tile(t, 512), _token_tile(l, 256)
    x_lat, x_ctx = x, ctx
    for layer in range(depth):
        na_l, g_l, z_l, ba_l = _inproj_call(x_lat, mods, w_in_b, layer, None, tm_lat)
        na_c, g_c, z_c, ba_c = _inproj_call(x_ctx, mods, w_in_b, layer, b, tm_ctx)

        na_out_l = _na_call(na_l, na_c, _na_bias_table(rpb[layer], win_r))

        qc, kc, vc = _gdn_prep_call(g_c, conv_w, layer, None, tm_ctx)
        gates_c = _gates_call(ba_c, alog_row, dtb_row, layer, gdn_heads, tm_ctx)
        of_c, ob_c, s_ctx = _gdn_scan_call(qc, kc, vc, gates_c, zeros_state)
        ql, kl, vl = _gdn_prep_call(g_l, conv_w, layer, rope_tabs, tm_lat)
        gates_l = _gates_call(ba_l, alog_row, dtb_row, layer, gdn_heads, tm_lat)
        of_l, ob_l, _ = _gdn_scan_call(ql, kl, vl, gates_l, s_ctx)

        x_lat = _outproj_call(na_out_l, of_l, ob_l, z_l, x_lat, mods, nw, w_out_b, ln1g, ln1b, layer, None,
                              tm_lat, alpha)
        x_lat = _mlp_call(x_lat, mods, w1_b, w2_b, ln2g, ln2b, layer, None, tm_lat, alpha)
        if layer < depth - 1:
            na_out_c = _ctx_attn_call(na_c)
            x_ctx = _outproj_call(na_out_c, of_c, ob_c, z_c, x_ctx, mods, nw, w_out_b, ln1g, ln1b, layer, b,
                                  tm_ctx, alpha)
            x_ctx = _mlp_call(x_ctx, mods, w1_b, w2_b, ln2g, ln2b, layer, b, tm_ctx, alpha)
    return x_lat
```

```python
import functools
import math

import jax
import jax.numpy as jnp
from jax import lax
from jax.experimental import pallas as pl
from jax.experimental.pallas import tpu as pltpu

GRID_W = 64
NA_HEAD_DIM = 64
NA_WIN_R = 8
NA_WIN_C = 16
GDN_HEAD_DIM = 128
GDN_CHUNK = 64
ROPE_BASE = 10000.0
N_MOD = 6
LN_EPS = 1e-5
NORM_EPS = 1e-6
MOD_ROWS = 16
BA_PAD = 128
VMEM_LIMIT = 56 * 1024 * 1024

BF16 = jnp.bfloat16
F32 = jnp.float32


def _params(*sem):
    return pltpu.CompilerParams(dimension_semantics=sem, vmem_limit_bytes=VMEM_LIMIT)


def _dot(a, b):
    return jnp.dot(a, b, preferred_element_type=F32)


def _dot_nt(a, b):
    return lax.dot_general(a, b, (((1,), (1,)), ((), ())), preferred_element_type=F32)


def _dot_tn(a, b):
    return lax.dot_general(a, b, (((0,), (0,)), ((), ())), preferred_element_type=F32)


def _silu(x):
    return x * jax.nn.sigmoid(x)


def _softplus(x):
    return jnp.maximum(x, 0.0) + jnp.log1p(jnp.exp(-jnp.abs(x)))


def _layer_norm(x, g, b):
    mu = jnp.mean(x, axis=-1, keepdims=True)
    xc = x - mu
    var = jnp.mean(xc * xc, axis=-1, keepdims=True)
    return xc * lax.rsqrt(var + LN_EPS) * g + b


def _ada_kernel(c_ref, w_ref, b_ref, o_ref):
    o_ref[...] = _dot(_silu(c_ref[...]), w_ref[...]) + b_ref[...]


def _ada_call(cc, w_ada, b_ada):
    depth, d, n = w_ada.shape
    tn = 1536
    return pl.pallas_call(
        _ada_kernel,
        grid=(depth, n // tn),
        in_specs=[pl.BlockSpec((MOD_ROWS, d), lambda l, j: (0, 0)),
                  pl.BlockSpec((None, d, tn), lambda l, j: (l, 0, j)),
                  pl.BlockSpec((None, 1, tn), lambda l, j: (l, 0, j))],
        out_specs=pl.BlockSpec((None, MOD_ROWS, tn), lambda l, j: (l, 0, j)),
        out_shape=jax.ShapeDtypeStruct((depth, MOD_ROWS, n), F32),
        compiler_params=_params("parallel", "parallel"),
        name="ada_mod",
    )(cc, w_ada, b_ada.reshape(depth, 1, n))


def _inproj_kernel(x_ref, mod_ref, w_ref, na_ref, g_ref, z_ref, ba_ref, *, na3, g3, zw):
    h = x_ref[...] * (1.0 + mod_ref[1:2, :]) + mod_ref[0:1, :]
    hb = h.astype(BF16)
    na_ref[...] = _dot(hb, w_ref[:, 0:na3]).astype(na_ref.dtype)
    g_ref[...] = _dot(hb, w_ref[:, na3:na3 + g3]).astype(g_ref.dtype)
    z_ref[...] = _dot(hb, w_ref[:, na3 + g3:na3 + g3 + zw]).astype(z_ref.dtype)
    ba_ref[...] = _dot(hb, w_ref[:, na3 + g3 + zw:])


def _inproj_call(x, mods, w_in_b, layer, mod_row, tm):
    b, t, d = x.shape
    na3, g3, zw = 3 * (d // 2), 3 * (d // 2), d // 2
    ncols = w_in_b.shape[-1]
    mod_idx = (lambda bi, ti: (layer, bi, 0, 0)) if mod_row is None else (lambda bi, ti: (layer, mod_row, 0, 0))
    return pl.pallas_call(
        functools.partial(_inproj_kernel, na3=na3, g3=g3, zw=zw),
        grid=(b, t // tm),
        in_specs=[pl.BlockSpec((None, tm, d), lambda bi, ti: (bi, ti, 0)),
                  pl.BlockSpec((None, None, N_MOD, d), mod_idx),
                  pl.BlockSpec((None, d, ncols), lambda bi, ti: (layer, 0, 0))],
        out_specs=[pl.BlockSpec((None, tm, na3), lambda bi, ti: (bi, ti, 0)),
                   pl.BlockSpec((None, tm, g3), lambda bi, ti: (bi, ti, 0)),
                   pl.BlockSpec((None, tm, zw), lambda bi, ti: (bi, ti, 0)),
                   pl.BlockSpec((None, tm, BA_PAD), lambda bi, ti: (bi, ti, 0))],
        out_shape=[jax.ShapeDtypeStruct((b, t, na3), BF16),
                   jax.ShapeDtypeStruct((b, t, g3), BF16),
                   jax.ShapeDtypeStruct((b, t, zw), BF16),
                   jax.ShapeDtypeStruct((b, t, BA_PAD), F32)],
        compiler_params=_params("parallel", "parallel"),
        name="in_proj",
    )(x, mods, w_in_b)


def _softmax_pv(s_list, v_list):
    m = None
    for s in s_list:
        sm = jnp.max(s, axis=-1, keepdims=True)
        m = sm if m is None else jnp.maximum(m, sm)
    den = None
    acc = None
    for s, v in zip(s_list, v_list):
        p = jnp.exp(s - m)
        ps = jnp.sum(p, axis=-1, keepdims=True)
        den = ps if den is None else den + ps
        pv = _dot(p.astype(BF16), v)
        acc = pv if acc is None else acc + pv
    return acc / den


def _na_kernel(q_ref, k_ref, v_ref, kc_ref, vc_ref, bias_ref, o_ref, *, heads, rows, win_r):
    r = pl.program_id(1)
    n_band = win_r * GRID_W
    row_start = jnp.clip(r - win_r // 2, 0, rows - win_r)
    koff = pl.multiple_of(row_start * GRID_W, GRID_W)
    scale = NA_HEAD_DIM ** -0.5
    pairs = range(heads // 2)
    gq = q_ref.shape[0]
    lane = lax.broadcasted_iota(jnp.int32, (gq, 2 * NA_HEAD_DIM), 1)
    qs, s_lat, s_ctx = [], [], []
    for pr in pairs:
        sl = slice(2 * pr * NA_HEAD_DIM, 2 * (pr + 1) * NA_HEAD_DIM)
        qp = q_ref[:, sl]
        zero = jnp.zeros_like(qp)
        qs.append(jnp.concatenate([jnp.where(lane < NA_HEAD_DIM, qp, zero),
                                   jnp.where(lane >= NA_HEAD_DIM, qp, zero)], axis=0))
    for pr in pairs:
        sl = slice(2 * pr * NA_HEAD_DIM, 2 * (pr + 1) * NA_HEAD_DIM)
        bias = bias_ref[2 * pr:2 * pr + 2].reshape(2 * gq, n_band)
        s_lat.append(_dot_nt(qs[pr], k_ref[pl.ds(koff, n_band), sl]) * scale + bias)
        s_ctx.append(_dot_nt(qs[pr], kc_ref[:, sl]) * scale)
    m = [jnp.maximum(jnp.max(s_lat[pr], axis=-1, keepdims=True), jnp.max(s_ctx[pr], axis=-1, keepdims=True))
         for pr in pairs]
    p_lat = [jnp.exp(s_lat[pr] - m[pr]) for pr in pairs]
    p_ctx = [jnp.exp(s_ctx[pr] - m[pr]) for pr in pairs]
    den = [jnp.sum(p_lat[pr], axis=-1, keepdims=True) + jnp.sum(p_ctx[pr], axis=-1, keepdims=True) for pr in pairs]
    outs = []
    for pr in pairs:
        sl = slice(2 * pr * NA_HEAD_DIM, 2 * (pr + 1) * NA_HEAD_DIM)
        pv = (_dot(p_lat[pr].astype(BF16), v_ref[pl.ds(koff, n_band), sl])
              + _dot(p_ctx[pr].astype(BF16), vc_ref[:, sl])) / den[pr]
        outs.append(jnp.where(lane < NA_HEAD_DIM, pv[:gq], pv[gq:]))
    o_ref[...] = jnp.concatenate(outs, axis=-1).astype(o_ref.dtype)


def _na_bias_table(rpb, win_r):
    cols = jnp.arange(GRID_W)
    col_start = jnp.clip(cols - NA_WIN_C // 2, 0, GRID_W - NA_WIN_C)
    col_in = (cols[None, :] >= col_start[:, None]) & (cols[None, :] < col_start[:, None] + NA_WIN_C)
    dc_idx = jnp.clip(cols[None, :] - cols[:, None], 1 - NA_WIN_C, NA_WIN_C - 1) + NA_WIN_C - 1
    rpb_cols = rpb[:, :, dc_idx].astype(F32)
    rpb_cols = jnp.where(col_in[None, None], rpb_cols, -jnp.inf)
    h = rpb.shape[0]
    variants = []
    for dv in range(win_r):
        dr0 = dv + NA_WIN_R - win_r
        band = rpb_cols[:, dr0:dr0 + win_r]
        variants.append(jnp.transpose(band, (0, 2, 1, 3)).reshape(h, GRID_W, win_r * GRID_W))
    return jnp.stack(variants)


def _na_call(na_lat, na_ctx, bias_tab):
    b, t, w3 = na_lat.shape
    w = w3 // 3
    heads = w // NA_HEAD_DIM
    l = na_ctx.shape[1]
    rows = t // GRID_W
    win_r = min(NA_WIN_R, rows)
    n_band = win_r * GRID_W

    def bias_idx(bi, r):
        row_start = jnp.clip(r - win_r // 2, 0, rows - win_r)
        return (row_start - r + win_r - 1, 0, 0, 0)

    return pl.pallas_call(
        functools.partial(_na_kernel, heads=heads, rows=rows, win_r=win_r),
        grid=(b, rows),
        in_specs=[pl.BlockSpec((None, GRID_W, w), lambda bi, r: (bi, r, 0)),
                  pl.BlockSpec((None, t, w), lambda bi, r: (bi, 0, 1)),
                  pl.BlockSpec((None, t, w), lambda bi, r: (bi, 0, 2)),
                  pl.BlockSpec((None, l, w), lambda bi, r: (bi, 0, 1)),
                  pl.BlockSpec((None, l, w), lambda bi, r: (bi, 0, 2)),
                  pl.BlockSpec((None, heads, GRID_W, n_band), bias_idx)],
        out_specs=pl.BlockSpec((None, GRID_W, w), lambda bi, r: (bi, r, 0)),
        out_shape=jax.ShapeDtypeStruct((b, t, w), BF16),
        compiler_params=_params("parallel", "arbitrary"),
        name="na_attn",
    )(na_lat, na_lat, na_lat, na_ctx, na_ctx, bias_tab)


def _ctx_attn_kernel(q_ref, k_ref, v_ref, o_ref, *, heads):
    scale = NA_HEAD_DIM ** -0.5
    outs = []
    for h in range(heads):
        sl = slice(h * NA_HEAD_DIM, (h + 1) * NA_HEAD_DIM)
        s = _dot_nt(q_ref[:, sl], k_ref[:, sl]) * scale
        outs.append(_softmax_pv([s], [v_ref[:, sl]]))
    o_ref[...] = jnp.concatenate(outs, axis=-1).astype(o_ref.dtype)


def _ctx_attn_call(na_ctx):
    b, l, w3 = na_ctx.shape
    w = w3 // 3
    return pl.pallas_call(
        functools.partial(_ctx_attn_kernel, heads=w // NA_HEAD_DIM),
        grid=(b,),
        in_specs=[pl.BlockSpec((None, l, w), lambda bi: (bi, 0, 0)),
                  pl.BlockSpec((None, l, w), lambda bi: (bi, 0, 1)),
                  pl.BlockSpec((None, l, w), lambda bi: (bi, 0, 2))],
        out_specs=pl.BlockSpec((None, l, w), lambda bi: (bi, 0, 0)),
        out_shape=jax.ShapeDtypeStruct((b, l, w), BF16),
        compiler_params=_params("parallel"),
        name="ctx_attn",
    )(na_ctx, na_ctx, na_ctx)


HALO = 16


def _conv_silu(prev_ref, x_ref, next_ref, w_ref, pad_ref, tt):
    ti = pl.program_id(2)
    nt = pl.num_programs(2)
    kw = w_ref.shape[0]
    half = kw // 2
    pad_ref[0:HALO, :] = jnp.where(ti > 0, prev_ref[...].astype(F32), 0.0)
    pad_ref[HALO + tt:2 * HALO + tt, :] = jnp.where(ti < nt - 1, next_ref[...].astype(F32), 0.0)
    pad_ref[HALO:HALO + tt, :] = x_ref[...].astype(F32)
    acc = None
    for j in range(kw):
        term = pad_ref[HALO - half + j:HALO - half + j + tt, :] * w_ref[j:j + 1, :]
        acc = term if acc is None else acc + term
    return _silu(acc)


def _l2norm(x):
    return x * lax.rsqrt(jnp.sum(x * x, axis=-1, keepdims=True) + NORM_EPS)


def _rope(x, cos, sin):
    quarter = GDN_HEAD_DIM // 4
    lane = lax.broadcasted_iota(jnp.int32, x.shape, 1)
    first = (lane % (2 * quarter)) < quarter
    partner = jnp.where(first, pltpu.roll(x, GDN_HEAD_DIM - quarter, 1), pltpu.roll(x, quarter, 1))
    return x * cos + partner * sin


def _gdn_prep_kernel(*refs, tt, use_rope):
    (qp, qx, qn, kp, kx, kn, vp, vx, vn, wq_ref, wk_ref, wv_ref), rest = refs[:12], refs[12:]
    if use_rope:
        cos_ref, sin_ref, q_ref, k_ref, v_ref, pad_q, pad_k, pad_v = rest
    else:
        q_ref, k_ref, v_ref, pad_q, pad_k, pad_v = rest
    q = _l2norm(_conv_silu(qp, qx, qn, wq_ref, pad_q, tt))
    k = _l2norm(_conv_silu(kp, kx, kn, wk_ref, pad_k, tt))
    v = _conv_silu(vp, vx, vn, wv_ref, pad_v, tt)
    if use_rope:
        q = _rope(q, cos_ref[...], sin_ref[...])
        k = _rope(k, cos_ref[...], sin_ref[...])
    q_ref[...] = (q * (GDN_HEAD_DIM ** -0.5)).astype(q_ref.dtype)
    k_ref[...] = k.astype(k_ref.dtype)
    v_ref[...] = v.astype(v_ref.dtype)


def _gdn_prep_call(gqkv, conv_w, layer, rope_tabs, tt):
    b, t, w3 = gqkv.shape
    w = w3 // 3
    heads = w // GDN_HEAD_DIM
    kw = conv_w.shape[1]
    use_rope = rope_tabs is not None
    hd = GDN_HEAD_DIM
    t8, n8 = tt // HALO, t // HALO

    in_specs, args = [], []
    for part in range(3):
        col = lambda h, part=part: part * heads + h
        in_specs += [
            pl.BlockSpec((None, HALO, hd), lambda bi, h, ti, col=col: (bi, jnp.maximum(ti * t8 - 1, 0), col(h))),
            pl.BlockSpec((None, tt, hd), lambda bi, h, ti, col=col: (bi, ti, col(h))),
            pl.BlockSpec((None, HALO, hd),
                         lambda bi, h, ti, col=col: (bi, jnp.minimum((ti + 1) * t8, n8 - 1), col(h))),
        ]
        args += [gqkv, gqkv, gqkv]
    for part in range(3):
        in_specs.append(pl.BlockSpec((None, kw, hd), lambda bi, h, ti, part=part: (layer, 0, part * heads + h)))
        args.append(conv_w)
    if use_rope:
        in_specs += [pl.BlockSpec((tt, hd), lambda bi, h, ti: (ti, 0))] * 2
        args += list(rope_tabs)
    ospec = pl.BlockSpec((None, tt, hd), lambda bi, h, ti: (bi, ti, h))
    return pl.pallas_call(
        functools.partial(_gdn_prep_kernel, tt=tt, use_rope=use_rope),
        grid=(b, heads, t // tt),
        in_specs=in_specs,
        out_specs=[ospec, ospec, ospec],
        out_shape=[jax.ShapeDtypeStruct((b, t, w), BF16)] * 3,
        scratch_shapes=[pltpu.VMEM((tt + 2 * HALO, hd), F32)] * 3,
        compiler_params=_params("parallel", "parallel", "parallel"),
        name="gdn_prep",
    )(*args)


def _rope_tables(t):
    pos = jnp.arange(t)
    row = (pos // GRID_W).astype(F32)
    col = (pos % GRID_W).astype(F32)
    axis_dim = GDN_HEAD_DIM // 2
    inv_freq = ROPE_BASE ** (-jnp.arange(0, axis_dim, 2, dtype=F32) / axis_dim)
    ang_r = row[:, None] * inv_freq[None, :]
    ang_c = col[:, None] * inv_freq[None, :]
    cos = jnp.concatenate([jnp.cos(ang_r), jnp.cos(ang_r), jnp.cos(ang_c), jnp.cos(ang_c)], axis=-1)
    sin = jnp.concatenate([-jnp.sin(ang_r), jnp.sin(ang_r), -jnp.sin(ang_c), jnp.sin(ang_c)], axis=-1)
    return cos, sin


def _gates(x, alog, dtb, heads):
    tt = x.shape[0]
    c = GDN_CHUNK
    lane = lax.broadcasted_iota(jnp.int32, x.shape, 1)
    row = lax.broadcasted_iota(jnp.int32, x.shape, 0) % c
    beta = jax.nn.sigmoid(x)
    g = -jnp.exp(alog) * _softplus(x + dtb)
    pre, suf = g, g
    s = 1
    while s < c:
        pre = pre + jnp.where(row >= s, pltpu.roll(pre, s, 0), 0.0)
        suf = suf + jnp.where(row < c - s, pltpu.roll(suf, tt - s, 0), 0.0)
        s *= 2
    gc = jnp.where(lane < 3 * heads, pre, suf)
    return jnp.where(lane < 2 * heads, beta, gc)


def _gdn_intra_kernel(q_ref, k_ref, v_ref, ba_ref, alog_ref, dtb_ref, uf_ref, ub_ref, wqf_ref, wqb_ref, kdf_ref,
                      kdb_ref, at_ref, cd_ref, *, heads):
    c, hd = GDN_CHUNK, GDN_HEAD_DIM
    assert 2 * c == hd
    tt = q_ref.shape[0]
    ri = lax.broadcasted_iota(jnp.int32, (c, hd), 0)
    lane = lax.broadcasted_iota(jnp.int32, (c, hd), 1)
    left = lane < c
    cj = jnp.where(left, lane, lane - c)
    ahead = jnp.where(left, ri - cj, cj - ri)
    incl = ahead >= 0
    strict = ahead > 0
    eye2 = (ri == cj).astype(F32)
    gates = _gates(ba_ref[...], alog_ref[...], dtb_ref[...], heads)

    def blockdiag(pk):
        zero = jnp.zeros_like(pk)
        return jnp.concatenate([jnp.where(left, pk, zero), jnp.where(left, zero, pk)], axis=0)

    def rows(ck):
        return slice(ck * c, (ck + 1) * c)

    def cols(h):
        return slice(h * hd, (h + 1) * hd)

    groups = [(ck, h) for ck in range(tt // c) for h in range(heads)]
    ng = range(len(groups))
    g_ck = [gates[rows(ck), :] for ck in range(tt // c)]
    gt_ck = [jnp.concatenate([g, g], axis=0).T for g in g_ck]

    q = [q_ref[rows(ck), cols(h)].astype(F32) for ck, h in groups]
    k = [k_ref[rows(ck), cols(h)].astype(F32) for ck, h in groups]
    v = [v_ref[rows(ck), cols(h)].astype(F32) for ck, h in groups]
    col = lambda ck, l: g_ck[ck][:, l:l + 1]
    beta = [(col(ck, h), col(ck, heads + h)) for ck, h in groups]
    gc = [(col(ck, 2 * heads + h), col(ck, 3 * heads + h)) for ck, h in groups]
    beta2 = [jnp.where(left, beta[gi][0], beta[gi][1]) for gi in ng]
    gc2 = [jnp.where(left, gc[gi][0], gc[gi][1]) for gi in ng]
    grow2 = [jnp.where(left[0:1], gt_ck[ck][2 * heads + h:2 * heads + h + 1, :],
                       gt_ck[ck][3 * heads + h:3 * heads + h + 1, :]) for ck, h in groups]
    decay = [jnp.exp(jnp.where(incl, gc2[gi] - grow2[gi], -jnp.inf)) for gi in ng]
    gram = [_dot_nt(jnp.concatenate([q[gi], k[gi]], axis=0).astype(BF16),
                    jnp.concatenate([k[gi], k[gi]], axis=0).astype(BF16)) for gi in ng]
    attn = [(gram[gi][:c] * decay[gi]).astype(BF16) for gi in ng]
    p = [-jnp.where(strict, gram[gi][c:] * beta2[gi] * decay[gi], 0.0) for gi in ng]
    inv = [eye2 + p[gi] for gi in ng]
    n_sq = int(math.log2(c)) - 1
    pbd = [blockdiag(p[gi].astype(BF16)) for gi in ng]
    p = [_dot(p[gi].astype(BF16), pbd[gi]) for gi in ng]
    for _ in range(n_sq - 1):
        pbd = [blockdiag(p[gi].astype(BF16)) for gi in ng]
        both = [_dot(jnp.concatenate([p[gi], inv[gi]], axis=0).astype(BF16), pbd[gi]) for gi in ng]
        p = [both[gi][:c] for gi in ng]
        inv = [inv[gi] + both[gi][c:] for gi in ng]
    inv = [inv[gi] + _dot(inv[gi].astype(BF16), blockdiag(p[gi].astype(BF16))) for gi in ng]

    eg = [(jnp.exp(gc[gi][0]), jnp.exp(gc[gi][1])) for gi in ng]
    g_last = [(gc[gi][0][c - 1:c, :], gc[gi][1][0:1, :]) for gi in ng]
    rhs = [jnp.concatenate([jnp.concatenate([v[gi] * beta[gi][d], (k[gi] * beta[gi][d]) * eg[gi][d]], axis=1)
                            for d in range(2)], axis=0).astype(BF16) for gi in ng]
    uw = [_dot(blockdiag(inv[gi].astype(BF16)), rhs[gi]) for gi in ng]
    for gi, (ck, h) in enumerate(groups):
        at_ref[rows(ck), cols(h)] = attn[gi]
        for d, (u_ref, wq_ref, kd_ref) in enumerate(((uf_ref, wqf_ref, kdf_ref), (ub_ref, wqb_ref, kdb_ref))):
            u_ref[rows(ck), cols(h)] = uw[gi][d * c:(d + 1) * c, :hd]
            wq_ref[ck, :, cols(h)] = jnp.concatenate([uw[gi][d * c:(d + 1) * c, hd:], q[gi] * eg[gi][d]],
                                                     axis=0).astype(BF16)
            kd_ref[rows(ck), cols(h)] = (k[gi] * jnp.exp(g_last[gi][d] - gc[gi][d])).astype(BF16)
            cd_ref[ck, d * heads + h:d * heads + h + 1, :] = jnp.broadcast_to(jnp.exp(g_last[gi][d]), (1, hd))


def _gdn_intra_call(q, k, v, ba, alog_row, dtb_row, layer, tt):
    b, t, w = q.shape
    heads = w // GDN_HEAD_DIM
    c, hd = GDN_CHUNK, GDN_HEAD_DIM
    n, nck = t // c, tt // c
    tok = lambda width: pl.BlockSpec((None, tt, width), lambda bi, ti: (bi, ti, 0))
    lay = pl.BlockSpec((None, 1, ba.shape[-1]), lambda bi, ti: (layer, 0, 0))
    wq_spec = pl.BlockSpec((None, nck, 2 * c, w), lambda bi, ti: (bi, ti, 0, 0))
    cd_spec = pl.BlockSpec((None, nck, 2 * heads, hd), lambda bi, ti: (bi, ti, 0, 0))
    tok_sds = lambda dt: jax.ShapeDtypeStruct((b, t, w), dt)
    wq_sds = jax.ShapeDtypeStruct((b, n, 2 * c, w), BF16)
    return pl.pallas_call(
        functools.partial(_gdn_intra_kernel, heads=heads),
        grid=(b, t // tt),
        in_specs=[tok(w), tok(w), tok(w), tok(ba.shape[-1]), lay, lay],
        out_specs=[tok(w), tok(w), wq_spec, wq_spec, tok(w), tok(w), tok(w), cd_spec],
        out_shape=[tok_sds(F32), tok_sds(F32), wq_sds, wq_sds, tok_sds(BF16), tok_sds(BF16), tok_sds(BF16),
                   jax.ShapeDtypeStruct((b, n, 2 * heads, hd), F32)],
        compiler_params=_params("parallel", "parallel"),
        name="gdn_intra",
    )(q, k, v, ba, alog_row, dtb_row)


def _gdn_scan_kernel(uf, wqf, kdf, atf, cdf, ub, wqb, kdb, atb, cdb, s0_ref, of_ref, ob_ref, sfin_ref, s_scr, *,
                     heads, bb):
    i = pl.program_id(1)
    n = pl.num_programs(1)
    c, hd = GDN_CHUNK, GDN_HEAD_DIM

    @pl.when(i == 0)
    def _():
        s_scr[...] = s0_ref[...]

    dirs = ((uf, wqf, kdf, atf, cdf, of_ref), (ub, wqb, kdb, atb, cdb, ob_ref))
    chains = [(bi, d, h) for bi in range(bb) for d in range(2) for h in range(heads)]
    cols = lambda h: slice(h * hd, (h + 1) * hd)
    zeros = jnp.zeros((c, hd), BF16)
    s_prev = {ch: s_scr[ch] for ch in chains}
    ws = {(bi, d, h): _dot(dirs[d][1][bi, :, cols(h)], s_prev[bi, d, h].astype(BF16)) for bi, d, h in chains}
    v_new = {(bi, d, h): (dirs[d][0][bi, :, cols(h)] - ws[bi, d, h][:c]).astype(BF16) for bi, d, h in chains}
    v_pad = {(bi, d, h): jnp.concatenate([v_new[bi, d, h], zeros] if d == 0 else [zeros, v_new[bi, d, h]], axis=0)
             for bi, d, h in chains}
    o = {(bi, d, h): ws[bi, d, h][c:] + _dot(dirs[d][3][bi, :, cols(h)], v_pad[bi, d, h]) for bi, d, h in chains}
    for bi, d, h in chains:
        r = d * heads + h
        s_scr[bi, d, h] = (s_prev[bi, d, h] * dirs[d][4][bi, r:r + 1, :]
                           + _dot_tn(dirs[d][2][bi, :, cols(h)], v_new[bi, d, h]))
        dirs[d][5][bi, :, cols(h)] = o[bi, d, h].astype(dirs[d][5].dtype)

    @pl.when(i == n - 1)
    def _():
        sfin_ref[...] = s_scr[...]


def _gdn_scan_call(intra, s0):
    uf, ub, wqf, wqb, kdf, kdb, at, cd = intra
    b, t, w = uf.shape
    heads = w // GDN_HEAD_DIM
    c, hd = GDN_CHUNK, GDN_HEAD_DIM
    n = t // c
    bb = 2 if b % 2 == 0 else 1
    fwd = lambda bi, i: (bi, i, 0)
    bwd = lambda bi, i: (bi, n - 1 - i, 0)
    fwd4 = lambda bi, i: (bi, i, 0, 0)
    bwd4 = lambda bi, i: (bi, n - 1 - i, 0, 0)

    def specs(tok_idx, chunk_idx):
        return [pl.BlockSpec((bb, c, w), tok_idx), pl.BlockSpec((bb, None, 2 * c, w), chunk_idx),
                pl.BlockSpec((bb, c, w), tok_idx), pl.BlockSpec((bb, c, w), tok_idx),
                pl.BlockSpec((bb, None, 2 * heads, hd), chunk_idx)]

    sspec = pl.BlockSpec((bb, 2, heads, hd, hd), lambda bi, i: (bi, 0, 0, 0, 0))
    return pl.pallas_call(
        functools.partial(_gdn_scan_kernel, heads=heads, bb=bb),
        grid=(b // bb, n),
        in_specs=specs(fwd, fwd4) + specs(bwd, bwd4) + [sspec],
        out_specs=[pl.BlockSpec((bb, c, w), fwd), pl.BlockSpec((bb, c, w), bwd), sspec],
        out_shape=[jax.ShapeDtypeStruct((b, t, w), BF16), jax.ShapeDtypeStruct((b, t, w), BF16),
                   jax.ShapeDtypeStruct((b, 2, heads, hd, hd), F32)],
        scratch_shapes=[pltpu.VMEM((bb, 2, heads, hd, hd), F32)],
        compiler_params=_params("parallel", "arbitrary"),
        name="gdn_scan",
    )(uf, wqf, kdf, at, cd, ub, wqb, kdb, at, cd, s0)


def _post_kernel(na_ref, of_ref, ob_ref, z_ref, x_ref, mod_ref, nw_ref, wo_ref, g1_ref, b1_ref, w1_ref, w2_ref,
                 g2_ref, b2_ref, o_ref, *, alpha, n_split):
    hd = GDN_HEAD_DIM
    o = of_ref[...].astype(F32) + ob_ref[...].astype(F32)
    z = z_ref[...].astype(F32)
    parts = []
    for h in range(o.shape[-1] // hd):
        oh = o[:, h * hd:(h + 1) * hd]
        oh = oh * lax.rsqrt(jnp.mean(oh * oh, axis=-1, keepdims=True) + NORM_EPS) * nw_ref[...]
        parts.append(oh * _silu(z[:, h * hd:(h + 1) * hd]))
    gdn = jnp.concatenate(parts, axis=1).astype(BF16)
    na_w = na_ref.shape[-1]
    y = _dot(na_ref[...], wo_ref[0:na_w, :]) + _dot(gdn, wo_ref[na_w:, :])
    x1 = _layer_norm(alpha * x_ref[...] + mod_ref[2:3, :] * y, g1_ref[...], b1_ref[...])

    hb = (x1 * (1.0 + mod_ref[4:5, :]) + mod_ref[3:4, :]).astype(BF16)
    ck = w1_ref.shape[1] // n_split
    acc = None
    for j in range(n_split):
        a = jnp.maximum(_dot(hb, w1_ref[:, j * ck:(j + 1) * ck]), 0.0)
        part = _dot((a * a).astype(BF16), w2_ref[j * ck:(j + 1) * ck, :])
        acc = part if acc is None else acc + part
    o_ref[...] = _layer_norm(alpha * x1 + mod_ref[5:6, :] * acc, g2_ref[...], b2_ref[...])


def _post_call(na_out, o_f, o_b, z, x, mods, nw, w_out_b, ln1, w1_b, w2_b, ln2, layer, mod_row, tm, alpha):
    b, t, d = x.shape
    na_w, gw = na_out.shape[-1], o_f.shape[-1]
    dff = w1_b.shape[-1]
    mod_idx = (lambda bi, ti: (layer, bi, 0, 0)) if mod_row is None else (lambda bi, ti: (layer, mod_row, 0, 0))
    tok = lambda width: pl.BlockSpec((None, tm, width), lambda bi, ti: (bi, ti, 0))
    lay = lambda *shape: pl.BlockSpec((None,) + shape, lambda bi, ti: (layer,) + (0,) * len(shape),
                                      pipeline_mode=pl.Buffered(1))
    return pl.pallas_call(
        functools.partial(_post_kernel, alpha=alpha, n_split=4),
        grid=(b, t // tm),
        in_specs=[tok(na_w), tok(gw), tok(gw), tok(gw), tok(d),
                  pl.BlockSpec((None, None, N_MOD, d), mod_idx),
                  lay(1, GDN_HEAD_DIM), lay(na_w + gw, d), lay(1, d), lay(1, d),
                  lay(d, dff), lay(dff, d), lay(1, d), lay(1, d)],
        out_specs=tok(d),
        out_shape=jax.ShapeDtypeStruct((b, t, d), F32),
        compiler_params=_params("parallel", "parallel"),
        name="post_mix_mlp",
    )(na_out, o_f, o_b, z, x, mods, nw, w_out_b, ln1[0], ln1[1], w1_b, w2_b, ln2[0], ln2[1])


def _token_tile(t, target):
    return target if t % target == 0 else t


def kernel(x, c, ctx, c_ctx, w_ada, b_ada, w_in, conv_w, a_log, dt_bias, gdn_norm_w, rpb, w_out, ln1_g, ln1_b,
           w_mlp1, w_mlp2, ln2_g, ln2_b):
    depth, d, _ = w_ada.shape
    b, t, _ = x.shape
    l = ctx.shape[1]
    gdn_heads = a_log.shape[-1]
    alpha = (2 * depth) ** 0.25
    assert b + 1 <= MOD_ROWS and t % GRID_W == 0 and t % GDN_CHUNK == 0 and l % GDN_CHUNK == 0

    cc = jnp.zeros((MOD_ROWS, d), F32).at[:b].set(c).at[b].set(c_ctx)
    mods = _ada_call(cc, w_ada, b_ada).reshape(depth, MOD_ROWS, N_MOD, d)
    w_in_b = jnp.pad(w_in, ((0, 0), (0, 0), (0, BA_PAD - 4 * gdn_heads))).astype(BF16)
    w_out_b = w_out.astype(BF16)
    w1_b = w_mlp1.astype(BF16)
    w2_b = w_mlp2.astype(BF16)
    gate_pad = ((0, 0), (0, 0), (2 * gdn_heads, BA_PAD - 4 * gdn_heads))
    alog_row = jnp.pad(a_log.reshape(depth, 1, 2 * gdn_heads), gate_pad)
    dtb_row = jnp.pad(dt_bias.reshape(depth, 1, 2 * gdn_heads), gate_pad)
    nw = gdn_norm_w.reshape(depth, 1, GDN_HEAD_DIM)
    ln1g, ln1b = ln1_g.reshape(depth, 1, d), ln1_b.reshape(depth, 1, d)
    ln2g, ln2b = ln2_g.reshape(depth, 1, d), ln2_b.reshape(depth, 1, d)
    rope_tabs = _rope_tables(t)
    win_r = min(NA_WIN_R, t // GRID_W)
    zeros_state = jnp.zeros((b, 2, gdn_heads, GDN_HEAD_DIM, GDN_HEAD_DIM), F32)

    tm_lat, tm_ctx = _token_tile(t, 512), _token_tile(l, 256)
    ti_lat, ti_ctx = _token_tile(t, 4 * GDN_CHUNK), _token_tile(l, 4 * GDN_CHUNK)
    x_lat, x_ctx = x, ctx
    for layer in range(depth):
        na_l, g_l, z_l, ba_l = _inproj_call(x_lat, mods, w_in_b, layer, None, tm_lat)
        na_c, g_c, z_c, ba_c = _inproj_call(x_ctx, mods, w_in_b, layer, b, tm_ctx)

        na_out_l = _na_call(na_l, na_c, _na_bias_table(rpb[layer], win_r))

        qc, kc, vc = _gdn_prep_call(g_c, conv_w, layer, None, tm_ctx)
        of_c, ob_c, s_ctx = _gdn_scan_call(_gdn_intra_call(qc, kc, vc, ba_c, alog_row, dtb_row, layer, ti_ctx),
                                           zeros_state)
        ql, kl, vl = _gdn_prep_call(g_l, conv_w, layer, rope_tabs, tm_lat)
        of_l, ob_l, _ = _gdn_scan_call(_gdn_intra_call(ql, kl, vl, ba_l, alog_row, dtb_row, layer, ti_lat), s_ctx)

        x_lat = _post_call(na_out_l, of_l, ob_l, z_l, x_lat, mods, nw, w_out_b, (ln1g, ln1b), w1_b, w2_b,
                           (ln2g, ln2b), layer, None, tm_lat, alpha)
        if layer < depth - 1:
            na_out_c = _ctx_attn_call(na_c)
            x_ctx = _post_call(na_out_c, of_c, ob_c, z_c, x_ctx, mods, nw, w_out_b, (ln1g, ln1b), w1_b, w2_b,
                               (ln2g, ln2b), layer, b, tm_ctx, alpha)
    return x_lat
```

```python
import functools
import math

import jax
import jax.numpy as jnp
from jax import lax
from jax.experimental import pallas as pl
from jax.experimental.pallas import tpu as pltpu

GRID_W = 64
NA_HEAD_DIM = 64
NA_WIN_R = 8
NA_WIN_C = 16
GDN_HEAD_DIM = 128
GDN_CHUNK = 64
ROPE_BASE = 10000.0
N_MOD = 6
LN_EPS = 1e-5
NORM_EPS = 1e-6
MOD_ROWS = 16
BA_PAD = 128
VMEM_LIMIT = 56 * 1024 * 1024

BF16 = jnp.bfloat16
F32 = jnp.float32


def _params(*sem):
    return pltpu.CompilerParams(dimension_semantics=sem, vmem_limit_bytes=VMEM_LIMIT)


def _dot(a, b):
    return jnp.dot(a, b, preferred_element_type=F32)


def _dot_nt(a, b):
    return lax.dot_general(a, b, (((1,), (1,)), ((), ())), preferred_element_type=F32)


def _dot_tn(a, b):
    return lax.dot_general(a, b, (((0,), (0,)), ((), ())), preferred_element_type=F32)


def _silu(x):
    return x * jax.nn.sigmoid(x)


def _softplus(x):
    return jnp.maximum(x, 0.0) + jnp.log1p(jnp.exp(-jnp.abs(x)))


def _layer_norm(x, g, b):
    mu = jnp.mean(x, axis=-1, keepdims=True)
    xc = x - mu
    var = jnp.mean(xc * xc, axis=-1, keepdims=True)
    return xc * lax.rsqrt(var + LN_EPS) * g + b


def _ada_kernel(c_ref, w_ref, b_ref, o_ref):
    o_ref[...] = _dot(_silu(c_ref[...]), w_ref[...]) + b_ref[...]


def _ada_call(cc, w_ada, b_ada):
    depth, d, n = w_ada.shape
    tn = 1536
    return pl.pallas_call(
        _ada_kernel,
        grid=(depth, n // tn),
        in_specs=[pl.BlockSpec((MOD_ROWS, d), lambda l, j: (0, 0)),
                  pl.BlockSpec((None, d, tn), lambda l, j: (l, 0, j)),
                  pl.BlockSpec((None, 1, tn), lambda l, j: (l, 0, j))],
        out_specs=pl.BlockSpec((None, MOD_ROWS, tn), lambda l, j: (l, 0, j)),
        out_shape=jax.ShapeDtypeStruct((depth, MOD_ROWS, n), F32),
        compiler_params=_params("parallel", "parallel"),
        name="ada_mod",
    )(cc, w_ada, b_ada.reshape(depth, 1, n))


def _softmax_pv(s_list, v_list):
    m = None
    for s in s_list:
        sm = jnp.max(s, axis=-1, keepdims=True)
        m = sm if m is None else jnp.maximum(m, sm)
    den = None
    acc = None
    for s, v in zip(s_list, v_list):
        p = jnp.exp(s - m)
        ps = jnp.sum(p, axis=-1, keepdims=True)
        den = ps if den is None else den + ps
        pv = _dot(p.astype(BF16), v)
        acc = pv if acc is None else acc + pv
    return acc / den


def _na_kernel(q_ref, k_ref, v_ref, kc_ref, vc_ref, bias_ref, o_ref, *, heads, rows, win_r):
    r = pl.program_id(1)
    n_band = win_r * GRID_W
    row_start = jnp.clip(r - win_r // 2, 0, rows - win_r)
    koff = pl.multiple_of(row_start * GRID_W, GRID_W)
    scale = NA_HEAD_DIM ** -0.5
    pairs = range(heads // 2)
    gq = q_ref.shape[0]
    lane = lax.broadcasted_iota(jnp.int32, (gq, 2 * NA_HEAD_DIM), 1)
    qs, s_lat, s_ctx = [], [], []
    for pr in pairs:
        sl = slice(2 * pr * NA_HEAD_DIM, 2 * (pr + 1) * NA_HEAD_DIM)
        qp = q_ref[:, sl]
        zero = jnp.zeros_like(qp)
        qs.append(jnp.concatenate([jnp.where(lane < NA_HEAD_DIM, qp, zero),
                                   jnp.where(lane >= NA_HEAD_DIM, qp, zero)], axis=0))
    for pr in pairs:
        sl = slice(2 * pr * NA_HEAD_DIM, 2 * (pr + 1) * NA_HEAD_DIM)
        bias = bias_ref[2 * pr:2 * pr + 2].reshape(2 * gq, n_band)
        s_lat.append(_dot_nt(qs[pr], k_ref[pl.ds(koff, n_band), sl]) * scale + bias)
        s_ctx.append(_dot_nt(qs[pr], kc_ref[:, sl]) * scale)
    m = [jnp.maximum(jnp.max(s_lat[pr], axis=-1, keepdims=True), jnp.max(s_ctx[pr], axis=-1, keepdims=True))
         for pr in pairs]
    p_lat = [jnp.exp(s_lat[pr] - m[pr]) for pr in pairs]
    p_ctx = [jnp.exp(s_ctx[pr] - m[pr]) for pr in pairs]
    den = [jnp.sum(p_lat[pr], axis=-1, keepdims=True) + jnp.sum(p_ctx[pr], axis=-1, keepdims=True) for pr in pairs]
    outs = []
    for pr in pairs:
        sl = slice(2 * pr * NA_HEAD_DIM, 2 * (pr + 1) * NA_HEAD_DIM)
        pv = (_dot(p_lat[pr].astype(BF16), v_ref[pl.ds(koff, n_band), sl])
              + _dot(p_ctx[pr].astype(BF16), vc_ref[:, sl])) / den[pr]
        outs.append(jnp.where(lane < NA_HEAD_DIM, pv[:gq], pv[gq:]))
    o_ref[...] = jnp.concatenate(outs, axis=-1).astype(o_ref.dtype)


def _na_bias_table(rpb, win_r):
    cols = jnp.arange(GRID_W)
    col_start = jnp.clip(cols - NA_WIN_C // 2, 0, GRID_W - NA_WIN_C)
    col_in = (cols[None, :] >= col_start[:, None]) & (cols[None, :] < col_start[:, None] + NA_WIN_C)
    dc_idx = jnp.clip(cols[None, :] - cols[:, None], 1 - NA_WIN_C, NA_WIN_C - 1) + NA_WIN_C - 1
    rpb_cols = rpb[:, :, dc_idx].astype(F32)
    rpb_cols = jnp.where(col_in[None, None], rpb_cols, -jnp.inf)
    h = rpb.shape[0]
    variants = []
    for dv in range(win_r):
        dr0 = dv + NA_WIN_R - win_r
        band = rpb_cols[:, dr0:dr0 + win_r]
        variants.append(jnp.transpose(band, (0, 2, 1, 3)).reshape(h, GRID_W, win_r * GRID_W))
    return jnp.stack(variants)


def _na_call(na_lat, na_ctx, bias_tab):
    b, t, w3 = na_lat.shape
    w = w3 // 3
    heads = w // NA_HEAD_DIM
    l = na_ctx.shape[1]
    rows = t // GRID_W
    win_r = min(NA_WIN_R, rows)
    n_band = win_r * GRID_W

    def bias_idx(bi, r):
        row_start = jnp.clip(r - win_r // 2, 0, rows - win_r)
        return (row_start - r + win_r - 1, 0, 0, 0)

    return pl.pallas_call(
        functools.partial(_na_kernel, heads=heads, rows=rows, win_r=win_r),
        grid=(b, rows),
        in_specs=[pl.BlockSpec((None, GRID_W, w), lambda bi, r: (bi, r, 0)),
                  pl.BlockSpec((None, t, w), lambda bi, r: (bi, 0, 1)),
                  pl.BlockSpec((None, t, w), lambda bi, r: (bi, 0, 2)),
                  pl.BlockSpec((None, l, w), lambda bi, r: (bi, 0, 1)),
                  pl.BlockSpec((None, l, w), lambda bi, r: (bi, 0, 2)),
                  pl.BlockSpec((None, heads, GRID_W, n_band), bias_idx)],
        out_specs=pl.BlockSpec((None, GRID_W, w), lambda bi, r: (bi, r, 0)),
        out_shape=jax.ShapeDtypeStruct((b, t, w), BF16),
        compiler_params=_params("parallel", "arbitrary"),
        name="na_attn",
    )(na_lat, na_lat, na_lat, na_ctx, na_ctx, bias_tab)


def _ctx_attn_kernel(q_ref, k_ref, v_ref, o_ref, *, heads):
    scale = NA_HEAD_DIM ** -0.5
    outs = []
    for h in range(heads):
        sl = slice(h * NA_HEAD_DIM, (h + 1) * NA_HEAD_DIM)
        s = _dot_nt(q_ref[:, sl], k_ref[:, sl]) * scale
        outs.append(_softmax_pv([s], [v_ref[:, sl]]))
    o_ref[...] = jnp.concatenate(outs, axis=-1).astype(o_ref.dtype)


def _ctx_attn_call(na_ctx):
    b, l, w3 = na_ctx.shape
    w = w3 // 3
    return pl.pallas_call(
        functools.partial(_ctx_attn_kernel, heads=w // NA_HEAD_DIM),
        grid=(b,),
        in_specs=[pl.BlockSpec((None, l, w), lambda bi: (bi, 0, 0)),
                  pl.BlockSpec((None, l, w), lambda bi: (bi, 0, 1)),
                  pl.BlockSpec((None, l, w), lambda bi: (bi, 0, 2))],
        out_specs=pl.BlockSpec((None, l, w), lambda bi: (bi, 0, 0)),
        out_shape=jax.ShapeDtypeStruct((b, l, w), BF16),
        compiler_params=_params("parallel"),
        name="ctx_attn",
    )(na_ctx, na_ctx, na_ctx)


HALO = 16


def _l2norm(x):
    return x * lax.rsqrt(jnp.sum(x * x, axis=-1, keepdims=True) + NORM_EPS)


def _rope(x, cos, sin):
    quarter = GDN_HEAD_DIM // 4
    lane = lax.broadcasted_iota(jnp.int32, x.shape, 1)
    first = (lane % (2 * quarter)) < quarter
    partner = jnp.where(first, pltpu.roll(x, GDN_HEAD_DIM - quarter, 1), pltpu.roll(x, quarter, 1))
    return x * cos + partner * sin


def _inproj_kernel(*refs, na3, g3, zw, use_rope):
    xp_ref, x_ref, xn_ref, mod_ref, w_ref, cw_ref = refs[:6]
    if use_rope:
        cos_ref, sin_ref = refs[6:8]
    na_ref, q_ref, k_ref, v_ref, z_ref, ba_ref, g_scr = refs[-7:]
    ti = pl.program_id(1)
    nt = pl.num_programs(1)
    tm = x_ref.shape[0]
    hd = GDN_HEAD_DIM
    scale, shift = 1.0 + mod_ref[1:2, :], mod_ref[0:1, :]
    modulate = lambda ref: (ref[...] * scale + shift).astype(BF16)
    hb = modulate(x_ref)
    h_prev = jnp.where(ti > 0, modulate(xp_ref), jnp.zeros((HALO, x_ref.shape[1]), BF16))
    h_next = jnp.where(ti < nt - 1, modulate(xn_ref), jnp.zeros((HALO, x_ref.shape[1]), BF16))
    h_ext = jnp.concatenate([h_prev, hb, h_next], axis=0)

    kw = cw_ref.shape[0]
    half = kw // 2
    heads = g3 // 3 // hd
    outs = (q_ref, k_ref, v_ref)
    mxu_w = 2 * hd
    for slab in range(g3 // mxu_w):
        g_slab = _dot(h_ext, w_ref[:, na3 + slab * mxu_w:na3 + (slab + 1) * mxu_w])
        for sub in range(mxu_w // hd):
            blk = slab * (mxu_w // hd) + sub
            part, h = blk // heads, blk % heads
            cs = slice(blk * hd, (blk + 1) * hd)
            g_scr[blk] = g_slab[:, sub * hd:(sub + 1) * hd]
            acc = None
            for j in range(kw):
                term = g_scr[blk, HALO - half + j:HALO - half + j + tm, :] * cw_ref[j:j + 1, cs]
                acc = term if acc is None else acc + term
            y = _silu(acc)
            if part < 2:
                y = _l2norm(y)
                if use_rope:
                    y = _rope(y, cos_ref[...], sin_ref[...])
            if part == 0:
                y = y * (hd ** -0.5)
            outs[part][:, h * hd:(h + 1) * hd] = y.astype(outs[part].dtype)
    na_ref[...] = _dot(hb, w_ref[:, 0:na3]).astype(na_ref.dtype)
    z_ref[...] = _dot(hb, w_ref[:, na3 + g3:na3 + g3 + zw]).astype(z_ref.dtype)
    ba_ref[...] = _dot(hb, w_ref[:, na3 + g3 + zw:])


def _inproj_call(x, mods, w_in_b, conv_w, rope_tabs, layer, mod_row, tm):
    b, t, d = x.shape
    na3, g3, zw = 3 * (d // 2), 3 * (d // 2), d // 2
    ncols = w_in_b.shape[-1]
    kw = conv_w.shape[1]
    use_rope = rope_tabs is not None
    th, nh = tm // HALO, t // HALO
    mod_idx = (lambda bi, ti: (layer, bi, 0, 0)) if mod_row is None else (lambda bi, ti: (layer, mod_row, 0, 0))
    tok = lambda width: pl.BlockSpec((None, tm, width), lambda bi, ti: (bi, ti, 0))
    lay = lambda *shape: pl.BlockSpec((None,) + shape, lambda bi, ti: (layer,) + (0,) * len(shape),
                                      pipeline_mode=pl.Buffered(1))
    in_specs = [pl.BlockSpec((None, HALO, d), lambda bi, ti: (bi, jnp.maximum(ti * th - 1, 0), 0)),
                tok(d),
                pl.BlockSpec((None, HALO, d), lambda bi, ti: (bi, jnp.minimum((ti + 1) * th, nh - 1), 0)),
                pl.BlockSpec((None, None, N_MOD, d), mod_idx), lay(d, ncols), lay(kw, g3)]
    args = [x, x, x, mods, w_in_b, conv_w]
    if use_rope:
        in_specs += [pl.BlockSpec((tm, GDN_HEAD_DIM), lambda bi, ti: (ti, 0))] * 2
        args += list(rope_tabs)
    return pl.pallas_call(
        functools.partial(_inproj_kernel, na3=na3, g3=g3, zw=zw, use_rope=use_rope),
        grid=(b, t // tm),
        in_specs=in_specs,
        out_specs=[tok(na3), tok(g3 // 3), tok(g3 // 3), tok(g3 // 3), tok(zw), tok(BA_PAD)],
        out_shape=[jax.ShapeDtypeStruct((b, t, na3), BF16)] + [jax.ShapeDtypeStruct((b, t, g3 // 3), BF16)] * 3
        + [jax.ShapeDtypeStruct((b, t, zw), BF16), jax.ShapeDtypeStruct((b, t, BA_PAD), F32)],
        scratch_shapes=[pltpu.VMEM((g3 // GDN_HEAD_DIM, tm + 2 * HALO, GDN_HEAD_DIM), F32)],
        compiler_params=_params("parallel", "parallel"),
        name="in_proj_conv",
    )(*args)


def _rope_tables(t):
    pos = jnp.arange(t)
    row = (pos // GRID_W).astype(F32)
    col = (pos % GRID_W).astype(F32)
    axis_dim = GDN_HEAD_DIM // 2
    inv_freq = ROPE_BASE ** (-jnp.arange(0, axis_dim, 2, dtype=F32) / axis_dim)
    ang_r = row[:, None] * inv_freq[None, :]
    ang_c = col[:, None] * inv_freq[None, :]
    cos = jnp.concatenate([jnp.cos(ang_r), jnp.cos(ang_r), jnp.cos(ang_c), jnp.cos(ang_c)], axis=-1)
    sin = jnp.concatenate([-jnp.sin(ang_r), jnp.sin(ang_r), -jnp.sin(ang_c), jnp.sin(ang_c)], axis=-1)
    return cos, sin


def _gates(x, alog, dtb, heads):
    tt = x.shape[0]
    c = GDN_CHUNK
    lane = lax.broadcasted_iota(jnp.int32, x.shape, 1)
    row = lax.broadcasted_iota(jnp.int32, x.shape, 0) % c
    beta = jax.nn.sigmoid(x)
    g = -jnp.exp(alog) * _softplus(x + dtb)
    pre, suf = g, g
    s = 1
    while s < c:
        pre = pre + jnp.where(row >= s, pltpu.roll(pre, s, 0), 0.0)
        suf = suf + jnp.where(row < c - s, pltpu.roll(suf, tt - s, 0), 0.0)
        s *= 2
    gc = jnp.where(lane < 3 * heads, pre, suf)
    return jnp.where(lane < 2 * heads, beta, gc)


def _gdn_intra_kernel(q_ref, k_ref, v_ref, ba_ref, alog_ref, dtb_ref, uf_ref, ub_ref, wqf_ref, wqb_ref, kdf_ref,
                      kdb_ref, at_ref, cd_ref, *, heads):
    c, hd = GDN_CHUNK, GDN_HEAD_DIM
    assert 2 * c == hd
    tt = q_ref.shape[0]
    ri = lax.broadcasted_iota(jnp.int32, (c, hd), 0)
    lane = lax.broadcasted_iota(jnp.int32, (c, hd), 1)
    left = lane < c
    cj = jnp.where(left, lane, lane - c)
    ahead = jnp.where(left, ri - cj, cj - ri)
    incl = ahead >= 0
    strict = ahead > 0
    eye2 = (ri == cj).astype(F32)
    gates = _gates(ba_ref[...], alog_ref[...], dtb_ref[...], heads)

    def blockdiag(pk):
        zero = jnp.zeros_like(pk)
        return jnp.concatenate([jnp.where(left, pk, zero), jnp.where(left, zero, pk)], axis=0)

    def rows(ck):
        return slice(ck * c, (ck + 1) * c)

    def cols(h):
        return slice(h * hd, (h + 1) * hd)

    groups = [(ck, h) for ck in range(tt // c) for h in range(heads)]
    ng = range(len(groups))
    g_ck = [gates[rows(ck), :] for ck in range(tt // c)]
    gt_ck = [jnp.concatenate([g, g], axis=0).T for g in g_ck]

    q = [q_ref[rows(ck), cols(h)].astype(F32) for ck, h in groups]
    k = [k_ref[rows(ck), cols(h)].astype(F32) for ck, h in groups]
    v = [v_ref[rows(ck), cols(h)].astype(F32) for ck, h in groups]
    col = lambda ck, l: g_ck[ck][:, l:l + 1]
    beta = [(col(ck, h), col(ck, heads + h)) for ck, h in groups]
    gc = [(col(ck, 2 * heads + h), col(ck, 3 * heads + h)) for ck, h in groups]
    beta2 = [jnp.where(left, beta[gi][0], beta[gi][1]) for gi in ng]
    gc2 = [jnp.where(left, gc[gi][0], gc[gi][1]) for gi in ng]
    grow2 = [jnp.where(left[0:1], gt_ck[ck][2 * heads + h:2 * heads + h + 1, :],
                       gt_ck[ck][3 * heads + h:3 * heads + h + 1, :]) for ck, h in groups]
    decay = [jnp.exp(jnp.where(incl, gc2[gi] - grow2[gi], -jnp.inf)) for gi in ng]
    gram = [_dot_nt(jnp.concatenate([q[gi], k[gi]], axis=0).astype(BF16),
                    jnp.concatenate([k[gi], k[gi]], axis=0).astype(BF16)) for gi in ng]
    attn = [(gram[gi][:c] * decay[gi]).astype(BF16) for gi in ng]
    p = [-jnp.where(strict, gram[gi][c:] * beta2[gi] * decay[gi], 0.0) for gi in ng]
    inv = [eye2 + p[gi] for gi in ng]
    n_sq = int(math.log2(c)) - 1
    pbd = [blockdiag(p[gi].astype(BF16)) for gi in ng]
    p = [_dot(p[gi].astype(BF16), pbd[gi]) for gi in ng]
    for _ in range(n_sq - 1):
        pbd = [blockdiag(p[gi].astype(BF16)) for gi in ng]
        both = [_dot(jnp.concatenate([p[gi], inv[gi]], axis=0).astype(BF16), pbd[gi]) for gi in ng]
        p = [both[gi][:c] for gi in ng]
        inv = [inv[gi] + both[gi][c:] for gi in ng]
    inv = [inv[gi] + _dot(inv[gi].astype(BF16), blockdiag(p[gi].astype(BF16))) for gi in ng]

    eg = [(jnp.exp(gc[gi][0]), jnp.exp(gc[gi][1])) for gi in ng]
    g_last = [(gc[gi][0][c - 1:c, :], gc[gi][1][0:1, :]) for gi in ng]
    rhs = [jnp.concatenate([jnp.concatenate([v[gi] * beta[gi][d], (k[gi] * beta[gi][d]) * eg[gi][d]], axis=1)
                            for d in range(2)], axis=0).astype(BF16) for gi in ng]
    uw = [_dot(blockdiag(inv[gi].astype(BF16)), rhs[gi]) for gi in ng]
    for gi, (ck, h) in enumerate(groups):
        at_ref[rows(ck), cols(h)] = attn[gi]
        for d, (u_ref, wq_ref, kd_ref) in enumerate(((uf_ref, wqf_ref, kdf_ref), (ub_ref, wqb_ref, kdb_ref))):
            u_ref[rows(ck), cols(h)] = uw[gi][d * c:(d + 1) * c, :hd]
            wq_ref[ck, :, cols(h)] = jnp.concatenate([uw[gi][d * c:(d + 1) * c, hd:], q[gi] * eg[gi][d]],
                                                     axis=0).astype(BF16)
            kd_ref[rows(ck), cols(h)] = (k[gi] * jnp.exp(g_last[gi][d] - gc[gi][d])).astype(BF16)
            cd_ref[ck, d * heads + h:d * heads + h + 1, :] = jnp.broadcast_to(jnp.exp(g_last[gi][d]), (1, hd))


def _gdn_intra_call(q, k, v, ba, alog_row, dtb_row, layer, tt):
    b, t, w = q.shape
    heads = w // GDN_HEAD_DIM
    c, hd = GDN_CHUNK, GDN_HEAD_DIM
    n, nck = t // c, tt // c
    tok = lambda width: pl.BlockSpec((None, tt, width), lambda bi, ti: (bi, ti, 0))
    lay = pl.BlockSpec((None, 1, ba.shape[-1]), lambda bi, ti: (layer, 0, 0))
    wq_spec = pl.BlockSpec((None, nck, 2 * c, w), lambda bi, ti: (bi, ti, 0, 0))
    cd_spec = pl.BlockSpec((None, nck, 2 * heads, hd), lambda bi, ti: (bi, ti, 0, 0))
    tok_sds = lambda dt: jax.ShapeDtypeStruct((b, t, w), dt)
    wq_sds = jax.ShapeDtypeStruct((b, n, 2 * c, w), BF16)
    return pl.pallas_call(
        functools.partial(_gdn_intra_kernel, heads=heads),
        grid=(b, t // tt),
        in_specs=[tok(w), tok(w), tok(w), tok(ba.shape[-1]), lay, lay],
        out_specs=[tok(w), tok(w), wq_spec, wq_spec, tok(w), tok(w), tok(w), cd_spec],
        out_shape=[tok_sds(F32), tok_sds(F32), wq_sds, wq_sds, tok_sds(BF16), tok_sds(BF16), tok_sds(BF16),
                   jax.ShapeDtypeStruct((b, n, 2 * heads, hd), F32)],
        compiler_params=_params("parallel", "parallel"),
        name="gdn_intra",
    )(q, k, v, ba, alog_row, dtb_row)


def _gdn_scan_kernel(uf, wqf, kdf, atf, cdf, ub, wqb, kdb, atb, cdb, s0_ref, of_ref, ob_ref, sfin_ref, s_scr, *,
                     heads, bb):
    i = pl.program_id(1)
    n = pl.num_programs(1)
    c, hd = GDN_CHUNK, GDN_HEAD_DIM

    @pl.when(i == 0)
    def _():
        s_scr[...] = s0_ref[...]

    dirs = ((uf, wqf, kdf, atf, cdf, of_ref), (ub, wqb, kdb, atb, cdb, ob_ref))
    chains = [(bi, d, h) for bi in range(bb) for d in range(2) for h in range(heads)]
    cols = lambda h: slice(h * hd, (h + 1) * hd)
    zeros = jnp.zeros((c, hd), BF16)
    s_prev = {ch: s_scr[ch] for ch in chains}
    ws = {(bi, d, h): _dot(dirs[d][1][bi, :, cols(h)], s_prev[bi, d, h].astype(BF16)) for bi, d, h in chains}
    v_new = {(bi, d, h): (dirs[d][0][bi, :, cols(h)] - ws[bi, d, h][:c]).astype(BF16) for bi, d, h in chains}
    v_pad = {(bi, d, h): jnp.concatenate([v_new[bi, d, h], zeros] if d == 0 else [zeros, v_new[bi, d, h]], axis=0)
             for bi, d, h in chains}
    o = {(bi, d, h): ws[bi, d, h][c:] + _dot(dirs[d][3][bi, :, cols(h)], v_pad[bi, d, h]) for bi, d, h in chains}
    for bi, d, h in chains:
        r = d * heads + h
        s_scr[bi, d, h] = (s_prev[bi, d, h] * dirs[d][4][bi, r:r + 1, :]
                           + _dot_tn(dirs[d][2][bi, :, cols(h)], v_new[bi, d, h]))
        dirs[d][5][bi, :, cols(h)] = o[bi, d, h].astype(dirs[d][5].dtype)

    @pl.when(i == n - 1)
    def _():
        sfin_ref[...] = s_scr[...]


def _gdn_scan_call(intra, s0):
    uf, ub, wqf, wqb, kdf, kdb, at, cd = intra
    b, t, w = uf.shape
    heads = w // GDN_HEAD_DIM
    c, hd = GDN_CHUNK, GDN_HEAD_DIM
    n = t // c
    bb = next(cand for cand in (4, 2, 1) if b % cand == 0)
    fwd = lambda bi, i: (bi, i, 0)
    bwd = lambda bi, i: (bi, n - 1 - i, 0)
    fwd4 = lambda bi, i: (bi, i, 0, 0)
    bwd4 = lambda bi, i: (bi, n - 1 - i, 0, 0)

    def specs(tok_idx, chunk_idx):
        return [pl.BlockSpec((bb, c, w), tok_idx), pl.BlockSpec((bb, None, 2 * c, w), chunk_idx),
                pl.BlockSpec((bb, c, w), tok_idx), pl.BlockSpec((bb, c, w), tok_idx),
                pl.BlockSpec((bb, None, 2 * heads, hd), chunk_idx)]

    sspec = pl.BlockSpec((bb, 2, heads, hd, hd), lambda bi, i: (bi, 0, 0, 0, 0))
    return pl.pallas_call(
        functools.partial(_gdn_scan_kernel, heads=heads, bb=bb),
        grid=(b // bb, n),
        in_specs=specs(fwd, fwd4) + specs(bwd, bwd4) + [sspec],
        out_specs=[pl.BlockSpec((bb, c, w), fwd), pl.BlockSpec((bb, c, w), bwd), sspec],
        out_shape=[jax.ShapeDtypeStruct((b, t, w), BF16), jax.ShapeDtypeStruct((b, t, w), BF16),
                   jax.ShapeDtypeStruct((b, 2, heads, hd, hd), F32)],
        scratch_shapes=[pltpu.VMEM((bb, 2, heads, hd, hd), F32)],
        compiler_params=_params("parallel", "arbitrary"),
        name="gdn_scan",
    )(uf, wqf, kdf, at, cd, ub, wqb, kdb, at, cd, s0)


def _post_kernel(na_ref, of_ref, ob_ref, z_ref, x_ref, mod_ref, nw_ref, wo_ref, g1_ref, b1_ref, w1_ref, w2_ref,
                 g2_ref, b2_ref, o_ref, *, alpha, n_split):
    hd = GDN_HEAD_DIM
    o = of_ref[...].astype(F32) + ob_ref[...].astype(F32)
    z = z_ref[...].astype(F32)
    parts = []
    for h in range(o.shape[-1] // hd):
        oh = o[:, h * hd:(h + 1) * hd]
        oh = oh * lax.rsqrt(jnp.mean(oh * oh, axis=-1, keepdims=True) + NORM_EPS) * nw_ref[...]
        parts.append(oh * _silu(z[:, h * hd:(h + 1) * hd]))
    gdn = jnp.concatenate(parts, axis=1).astype(BF16)
    na_w = na_ref.shape[-1]
    y = _dot(na_ref[...], wo_ref[0:na_w, :]) + _dot(gdn, wo_ref[na_w:, :])
    x1 = _layer_norm(alpha * x_ref[...] + mod_ref[2:3, :] * y, g1_ref[...], b1_ref[...])

    hb = (x1 * (1.0 + mod_ref[4:5, :]) + mod_ref[3:4, :]).astype(BF16)
    ck = w1_ref.shape[1] // n_split
    acc = None
    for j in range(n_split):
        a = jnp.maximum(_dot(hb, w1_ref[:, j * ck:(j + 1) * ck]), 0.0)
        part = _dot((a * a).astype(BF16), w2_ref[j * ck:(j + 1) * ck, :])
        acc = part if acc is None else acc + part
    o_ref[...] = _layer_norm(alpha * x1 + mod_ref[5:6, :] * acc, g2_ref[...], b2_ref[...])


def _post_call(na_out, o_f, o_b, z, x, mods, nw, w_out_b, ln1, w1_b, w2_b, ln2, layer, mod_row, tm, alpha):
    b, t, d = x.shape
    na_w, gw = na_out.shape[-1], o_f.shape[-1]
    dff = w1_b.shape[-1]
    mod_idx = (lambda bi, ti: (layer, bi, 0, 0)) if mod_row is None else (lambda bi, ti: (layer, mod_row, 0, 0))
    tok = lambda width: pl.BlockSpec((None, tm, width), lambda bi, ti: (bi, ti, 0))
    lay = lambda *shape: pl.BlockSpec((None,) + shape, lambda bi, ti: (layer,) + (0,) * len(shape),
                                      pipeline_mode=pl.Buffered(1))
    return pl.pallas_call(
        functools.partial(_post_kernel, alpha=alpha, n_split=4),
        grid=(b, t // tm),
        in_specs=[tok(na_w), tok(gw), tok(gw), tok(gw), tok(d),
                  pl.BlockSpec((None, None, N_MOD, d), mod_idx),
                  lay(1, GDN_HEAD_DIM), lay(na_w + gw, d), lay(1, d), lay(1, d),
                  lay(d, dff), lay(dff, d), lay(1, d), lay(1, d)],
        out_specs=tok(d),
        out_shape=jax.ShapeDtypeStruct((b, t, d), F32),
        compiler_params=_params("parallel", "parallel"),
        name="post_mix_mlp",
    )(na_out, o_f, o_b, z, x, mods, nw, w_out_b, ln1[0], ln1[1], w1_b, w2_b, ln2[0], ln2[1])


def _token_tile(t, target):
    return target if t % target == 0 else t


def kernel(x, c, ctx, c_ctx, w_ada, b_ada, w_in, conv_w, a_log, dt_bias, gdn_norm_w, rpb, w_out, ln1_g, ln1_b,
           w_mlp1, w_mlp2, ln2_g, ln2_b):
    depth, d, _ = w_ada.shape
    b, t, _ = x.shape
    l = ctx.shape[1]
    gdn_heads = a_log.shape[-1]
    alpha = (2 * depth) ** 0.25
    assert b + 1 <= MOD_ROWS and t % GRID_W == 0 and t % GDN_CHUNK == 0 and l % GDN_CHUNK == 0

    cc = jnp.zeros((MOD_ROWS, d), F32).at[:b].set(c).at[b].set(c_ctx)
    mods = _ada_call(cc, w_ada, b_ada).reshape(depth, MOD_ROWS, N_MOD, d)
    w_in_b = jnp.pad(w_in, ((0, 0), (0, 0), (0, BA_PAD - 4 * gdn_heads))).astype(BF16)
    w_out_b = w_out.astype(BF16)
    w1_b = w_mlp1.astype(BF16)
    w2_b = w_mlp2.astype(BF16)
    gate_pad = ((0, 0), (0, 0), (2 * gdn_heads, BA_PAD - 4 * gdn_heads))
    alog_row = jnp.pad(a_log.reshape(depth, 1, 2 * gdn_heads), gate_pad)
    dtb_row = jnp.pad(dt_bias.reshape(depth, 1, 2 * gdn_heads), gate_pad)
    nw = gdn_norm_w.reshape(depth, 1, GDN_HEAD_DIM)
    ln1g, ln1b = ln1_g.reshape(depth, 1, d), ln1_b.reshape(depth, 1, d)
    ln2g, ln2b = ln2_g.reshape(depth, 1, d), ln2_b.reshape(depth, 1, d)
    rope_tabs = _rope_tables(t)
    win_r = min(NA_WIN_R, t // GRID_W)
    zeros_state = jnp.zeros((b, 2, gdn_heads, GDN_HEAD_DIM, GDN_HEAD_DIM), F32)

    tm_lat, tm_ctx = _token_tile(t, 512), _token_tile(l, 256)
    ti_lat, ti_ctx = _token_tile(t, 4 * GDN_CHUNK), _token_tile(l, 4 * GDN_CHUNK)
    x_lat, x_ctx = x, ctx
    for layer in range(depth):
        na_l, ql, kl, vl, z_l, ba_l = _inproj_call(x_lat, mods, w_in_b, conv_w, rope_tabs, layer, None, tm_lat)
        na_c, qc, kc, vc, z_c, ba_c = _inproj_call(x_ctx, mods, w_in_b, conv_w, None, layer, b, tm_ctx)

        na_out_l = _na_call(na_l, na_c, _na_bias_table(rpb[layer], win_r))

        of_c, ob_c, s_ctx = _gdn_scan_call(_gdn_intra_call(qc, kc, vc, ba_c, alog_row, dtb_row, layer, ti_ctx),
                                           zeros_state)
        of_l, ob_l, _ = _gdn_scan_call(_gdn_intra_call(ql, kl, vl, ba_l, alog_row, dtb_row, layer, ti_lat), s_ctx)

        x_lat = _post_call(na_out_l, of_l, ob_l, z_l, x_lat, mods, nw, w_out_b, (ln1g, ln1b), w1_b, w2_b,
                           (ln2g, ln2b), layer, None, tm_lat, alpha)
        if layer < depth - 1:
            na_out_c = _ctx_attn_call(na_c)
            x_ctx = _post_call(na_out_c, of_c, ob_c, z_c, x_ctx, mods, nw, w_out_b, (ln1g, ln1b), w1_b, w2_b,
                               (ln2g, ln2b), layer, b, tm_ctx, alpha)
    return x_lat
```

```python
import functools
import math

import jax
import jax.numpy as jnp
from jax import lax
from jax.experimental import pallas as pl
from jax.experimental.pallas import tpu as pltpu

GRID_W = 64
NA_HEAD_DIM = 64
NA_WIN_R = 8
NA_WIN_C = 16
GDN_HEAD_DIM = 128
GDN_CHUNK = 64
ROPE_BASE = 10000.0
N_MOD = 6
LN_EPS = 1e-5
NORM_EPS = 1e-6
MOD_ROWS = 16
BA_PAD = 128
VMEM_LIMIT = 56 * 1024 * 1024

BF16 = jnp.bfloat16
F32 = jnp.float32


def _params(*sem):
    return pltpu.CompilerParams(dimension_semantics=sem, vmem_limit_bytes=VMEM_LIMIT)


def _dot(a, b):
    return jnp.dot(a, b, preferred_element_type=F32)


def _dot_nt(a, b):
    return lax.dot_general(a, b, (((1,), (1,)), ((), ())), preferred_element_type=F32)


def _dot_tn(a, b):
    return lax.dot_general(a, b, (((0,), (0,)), ((), ())), preferred_element_type=F32)


def _silu(x):
    return x * jax.nn.sigmoid(x)


def _softplus(x):
    return jnp.maximum(x, 0.0) + jnp.log1p(jnp.exp(-jnp.abs(x)))


def _layer_norm(x, g, b):
    mu = jnp.mean(x, axis=-1, keepdims=True)
    xc = x - mu
    var = jnp.mean(xc * xc, axis=-1, keepdims=True)
    return xc * lax.rsqrt(var + LN_EPS) * g + b


def _ada_kernel(c_ref, w_ref, b_ref, o_ref):
    o_ref[...] = _dot(_silu(c_ref[...]), w_ref[...]) + b_ref[...]


def _ada_call(cc, w_ada, b_ada):
    depth, d, n = w_ada.shape
    tn = 1536
    return pl.pallas_call(
        _ada_kernel,
        grid=(depth, n // tn),
        in_specs=[pl.BlockSpec((MOD_ROWS, d), lambda l, j: (0, 0)),
                  pl.BlockSpec((None, d, tn), lambda l, j: (l, 0, j)),
                  pl.BlockSpec((None, 1, tn), lambda l, j: (l, 0, j))],
        out_specs=pl.BlockSpec((None, MOD_ROWS, tn), lambda l, j: (l, 0, j)),
        out_shape=jax.ShapeDtypeStruct((depth, MOD_ROWS, n), F32),
        compiler_params=_params("parallel", "parallel"),
        name="ada_mod",
    )(cc, w_ada, b_ada.reshape(depth, 1, n))


def _softmax_pv(s_list, v_list):
    m = None
    for s in s_list:
        sm = jnp.max(s, axis=-1, keepdims=True)
        m = sm if m is None else jnp.maximum(m, sm)
    den = None
    acc = None
    for s, v in zip(s_list, v_list):
        p = jnp.exp(s - m)
        ps = jnp.sum(p, axis=-1, keepdims=True)
        den = ps if den is None else den + ps
        pv = _dot(p.astype(BF16), v)
        acc = pv if acc is None else acc + pv
    return acc / den


def _na_kernel(q_ref, k_ref, v_ref, kc_ref, vc_ref, bias_ref, o_ref, *, heads, rows, win_r, rq):
    n_band = win_r * GRID_W
    scale = NA_HEAD_DIM ** -0.5
    pairs = range(heads // 2)
    gq = GRID_W
    lane = lax.broadcasted_iota(jnp.int32, (gq, 2 * NA_HEAD_DIM), 1)
    psl = lambda pr: slice(2 * pr * NA_HEAD_DIM, 2 * (pr + 1) * NA_HEAD_DIM)

    def scores(j):
        r = pl.program_id(1) * rq + j
        row_start = jnp.clip(r - win_r // 2, 0, rows - win_r)
        koff = pl.multiple_of(row_start * GRID_W, GRID_W)
        variant = row_start - r + win_r - 1
        s_lat, s_ctx = [], []
        for pr in pairs:
            qp = q_ref[j * gq:(j + 1) * gq, psl(pr)] * scale
            zero = jnp.zeros_like(qp)
            qs = jnp.concatenate([jnp.where(lane < NA_HEAD_DIM, qp, zero),
                                  jnp.where(lane >= NA_HEAD_DIM, qp, zero)], axis=0)
            bias = bias_ref[variant, 2 * pr:2 * pr + 2].reshape(2 * gq, n_band)
            s_lat.append(_dot_nt(qs, k_ref[pl.ds(koff, n_band), psl(pr)]) + bias)
            s_ctx.append(_dot_nt(qs, kc_ref[:, psl(pr)]))
        return koff, s_lat, s_ctx

    def softmax(st):
        koff, s_lat, s_ctx = st
        m = [jnp.maximum(jnp.max(s_lat[pr], axis=-1, keepdims=True), jnp.max(s_ctx[pr], axis=-1, keepdims=True))
             for pr in pairs]
        p_lat = [jnp.exp(s_lat[pr] - m[pr]) for pr in pairs]
        p_ctx = [jnp.exp(s_ctx[pr] - m[pr]) for pr in pairs]
        den = [jnp.sum(p_lat[pr], axis=-1, keepdims=True) + jnp.sum(p_ctx[pr], axis=-1, keepdims=True)
               for pr in pairs]
        return koff, [p.astype(BF16) for p in p_lat], [p.astype(BF16) for p in p_ctx], den

    def pv(j, st):
        koff, p_lat, p_ctx, den = st
        outs = []
        for pr in pairs:
            acc = (_dot(p_lat[pr], v_ref[pl.ds(koff, n_band), psl(pr)]) + _dot(p_ctx[pr], vc_ref[:, psl(pr)]))
            acc = acc / den[pr]
            outs.append(jnp.where(lane < NA_HEAD_DIM, acc[:gq], acc[gq:]))
        o_ref[j * gq:(j + 1) * gq, :] = jnp.concatenate(outs, axis=-1).astype(o_ref.dtype)

    pending = scores(0)
    for j in range(rq):
        nxt = scores(j + 1) if j + 1 < rq else None
        pv(j, softmax(pending))
        pending = nxt


def _na_bias_table(rpb, win_r):
    cols = jnp.arange(GRID_W)
    col_start = jnp.clip(cols - NA_WIN_C // 2, 0, GRID_W - NA_WIN_C)
    col_in = (cols[None, :] >= col_start[:, None]) & (cols[None, :] < col_start[:, None] + NA_WIN_C)
    dc_idx = jnp.clip(cols[None, :] - cols[:, None], 1 - NA_WIN_C, NA_WIN_C - 1) + NA_WIN_C - 1
    rpb_cols = rpb[:, :, dc_idx].astype(F32)
    rpb_cols = jnp.where(col_in[None, None], rpb_cols, -jnp.inf)
    h = rpb.shape[0]
    variants = []
    for dv in range(win_r):
        dr0 = dv + NA_WIN_R - win_r
        band = rpb_cols[:, dr0:dr0 + win_r]
        variants.append(jnp.transpose(band, (0, 2, 1, 3)).reshape(h, GRID_W, win_r * GRID_W))
    return jnp.stack(variants)


def _na_call(na_lat, na_ctx, bias_tab):
    b, t, w3 = na_lat.shape
    w = w3 // 3
    heads = w // NA_HEAD_DIM
    l = na_ctx.shape[1]
    rows = t // GRID_W
    win_r = min(NA_WIN_R, rows)
    n_band = win_r * GRID_W
    rq = 2 if rows % 2 == 0 else 1
    return pl.pallas_call(
        functools.partial(_na_kernel, heads=heads, rows=rows, win_r=win_r, rq=rq),
        grid=(b, rows // rq),
        in_specs=[pl.BlockSpec((None, rq * GRID_W, w), lambda bi, r: (bi, r, 0)),
                  pl.BlockSpec((None, t, w), lambda bi, r: (bi, 0, 1)),
                  pl.BlockSpec((None, t, w), lambda bi, r: (bi, 0, 2)),
                  pl.BlockSpec((None, l, w), lambda bi, r: (bi, 0, 1)),
                  pl.BlockSpec((None, l, w), lambda bi, r: (bi, 0, 2)),
                  pl.BlockSpec((win_r, heads, GRID_W, n_band), lambda bi, r: (0, 0, 0, 0),
                               pipeline_mode=pl.Buffered(1))],
        out_specs=pl.BlockSpec((None, rq * GRID_W, w), lambda bi, r: (bi, r, 0)),
        out_shape=jax.ShapeDtypeStruct((b, t, w), BF16),
        compiler_params=_params("parallel", "arbitrary"),
        name="na_attn",
    )(na_lat, na_lat, na_lat, na_ctx, na_ctx, bias_tab)


def _ctx_attn_kernel(q_ref, k_ref, v_ref, o_ref, *, heads):
    scale = NA_HEAD_DIM ** -0.5
    outs = []
    for h in range(heads):
        sl = slice(h * NA_HEAD_DIM, (h + 1) * NA_HEAD_DIM)
        s = _dot_nt(q_ref[:, sl], k_ref[:, sl]) * scale
        outs.append(_softmax_pv([s], [v_ref[:, sl]]))
    o_ref[...] = jnp.concatenate(outs, axis=-1).astype(o_ref.dtype)


def _ctx_attn_call(na_ctx):
    b, l, w3 = na_ctx.shape
    w = w3 // 3
    return pl.pallas_call(
        functools.partial(_ctx_attn_kernel, heads=w // NA_HEAD_DIM),
        grid=(b,),
        in_specs=[pl.BlockSpec((None, l, w), lambda bi: (bi, 0, 0)),
                  pl.BlockSpec((None, l, w), lambda bi: (bi, 0, 1)),
                  pl.BlockSpec((None, l, w), lambda bi: (bi, 0, 2))],
        out_specs=pl.BlockSpec((None, l, w), lambda bi: (bi, 0, 0)),
        out_shape=jax.ShapeDtypeStruct((b, l, w), BF16),
        compiler_params=_params("parallel"),
        name="ctx_attn",
    )(na_ctx, na_ctx, na_ctx)


HALO = 16


def _l2norm(x):
    return x * lax.rsqrt(jnp.sum(x * x, axis=-1, keepdims=True) + NORM_EPS)


def _rope(x, cos, sin):
    quarter = GDN_HEAD_DIM // 4
    lane = lax.broadcasted_iota(jnp.int32, x.shape, 1)
    first = (lane % (2 * quarter)) < quarter
    partner = jnp.where(first, pltpu.roll(x, GDN_HEAD_DIM - quarter, 1), pltpu.roll(x, quarter, 1))
    return x * cos + partner * sin


def _inproj_kernel(*refs, na3, g3, zw, use_rope):
    xp_ref, x_ref, xn_ref, mod_ref, w_ref, cw_ref = refs[:6]
    if use_rope:
        cos_ref, sin_ref = refs[6:8]
    na_ref, q_ref, k_ref, v_ref, z_ref, ba_ref, g_scr = refs[-7:]
    ti = pl.program_id(1)
    nt = pl.num_programs(1)
    tm = x_ref.shape[0]
    hd = GDN_HEAD_DIM
    scale, shift = 1.0 + mod_ref[1:2, :], mod_ref[0:1, :]
    modulate = lambda ref: (ref[...] * scale + shift).astype(BF16)
    hb = modulate(x_ref)
    h_prev = jnp.where(ti > 0, modulate(xp_ref), jnp.zeros((HALO, x_ref.shape[1]), BF16))
    h_next = jnp.where(ti < nt - 1, modulate(xn_ref), jnp.zeros((HALO, x_ref.shape[1]), BF16))
    h_ext = jnp.concatenate([h_prev, hb, h_next], axis=0)

    kw = cw_ref.shape[0]
    half = kw // 2
    heads = g3 // 3 // hd
    outs = (q_ref, k_ref, v_ref)
    mxu_w = 2 * hd
    for slab in range(g3 // mxu_w):
        g_slab = _dot(h_ext, w_ref[:, na3 + slab * mxu_w:na3 + (slab + 1) * mxu_w])
        for sub in range(mxu_w // hd):
            blk = slab * (mxu_w // hd) + sub
            part, h = blk // heads, blk % heads
            cs = slice(blk * hd, (blk + 1) * hd)
            g_scr[blk] = g_slab[:, sub * hd:(sub + 1) * hd]
            acc = None
            for j in range(kw):
                term = g_scr[blk, HALO - half + j:HALO - half + j + tm, :] * cw_ref[j:j + 1, cs]
                acc = term if acc is None else acc + term
            y = _silu(acc)
            if part < 2:
                y = _l2norm(y)
                if use_rope:
                    y = _rope(y, cos_ref[...], sin_ref[...])
            if part == 0:
                y = y * (hd ** -0.5)
            outs[part][:, h * hd:(h + 1) * hd] = y.astype(outs[part].dtype)
    na_ref[...] = _dot(hb, w_ref[:, 0:na3]).astype(na_ref.dtype)
    z_ref[...] = _dot(hb, w_ref[:, na3 + g3:na3 + g3 + zw]).astype(z_ref.dtype)
    ba_ref[...] = _dot(hb, w_ref[:, na3 + g3 + zw:])


def _inproj_call(x, mods, w_in_b, conv_w, rope_tabs, layer, mod_row, tm):
    b, t, d = x.shape
    na3, g3, zw = 3 * (d // 2), 3 * (d // 2), d // 2
    ncols = w_in_b.shape[-1]
    kw = conv_w.shape[1]
    use_rope = rope_tabs is not None
    th, nh = tm // HALO, t // HALO
    mod_idx = (lambda bi, ti: (layer, bi, 0, 0)) if mod_row is None else (lambda bi, ti: (layer, mod_row, 0, 0))
    tok = lambda width: pl.BlockSpec((None, tm, width), lambda bi, ti: (bi, ti, 0))
    lay = lambda *shape: pl.BlockSpec((None,) + shape, lambda bi, ti: (layer,) + (0,) * len(shape),
                                      pipeline_mode=pl.Buffered(1))
    in_specs = [pl.BlockSpec((None, HALO, d), lambda bi, ti: (bi, jnp.maximum(ti * th - 1, 0), 0)),
                tok(d),
                pl.BlockSpec((None, HALO, d), lambda bi, ti: (bi, jnp.minimum((ti + 1) * th, nh - 1), 0)),
                pl.BlockSpec((None, None, N_MOD, d), mod_idx), lay(d, ncols), lay(kw, g3)]
    args = [x, x, x, mods, w_in_b, conv_w]
    if use_rope:
        in_specs += [pl.BlockSpec((tm, GDN_HEAD_DIM), lambda bi, ti: (ti, 0))] * 2
        args += list(rope_tabs)
    return pl.pallas_call(
        functools.partial(_inproj_kernel, na3=na3, g3=g3, zw=zw, use_rope=use_rope),
        grid=(b, t // tm),
        in_specs=in_specs,
        out_specs=[tok(na3), tok(g3 // 3), tok(g3 // 3), tok(g3 // 3), tok(zw), tok(BA_PAD)],
        out_shape=[jax.ShapeDtypeStruct((b, t, na3), BF16)] + [jax.ShapeDtypeStruct((b, t, g3 // 3), BF16)] * 3
        + [jax.ShapeDtypeStruct((b, t, zw), BF16), jax.ShapeDtypeStruct((b, t, BA_PAD), F32)],
        scratch_shapes=[pltpu.VMEM((g3 // GDN_HEAD_DIM, tm + 2 * HALO, GDN_HEAD_DIM), F32)],
        compiler_params=_params("parallel", "parallel"),
        name="in_proj_conv",
    )(*args)


def _rope_tables(t):
    pos = jnp.arange(t)
    row = (pos // GRID_W).astype(F32)
    col = (pos % GRID_W).astype(F32)
    axis_dim = GDN_HEAD_DIM // 2
    inv_freq = ROPE_BASE ** (-jnp.arange(0, axis_dim, 2, dtype=F32) / axis_dim)
    ang_r = row[:, None] * inv_freq[None, :]
    ang_c = col[:, None] * inv_freq[None, :]
    cos = jnp.concatenate([jnp.cos(ang_r), jnp.cos(ang_r), jnp.cos(ang_c), jnp.cos(ang_c)], axis=-1)
    sin = jnp.concatenate([-jnp.sin(ang_r), jnp.sin(ang_r), -jnp.sin(ang_c), jnp.sin(ang_c)], axis=-1)
    return cos, sin


def _gates(x, alog, dtb, heads):
    tt = x.shape[0]
    c = GDN_CHUNK
    lane = lax.broadcasted_iota(jnp.int32, x.shape, 1)
    row = lax.broadcasted_iota(jnp.int32, x.shape, 0) % c
    beta = jax.nn.sigmoid(x)
    g = -jnp.exp(alog) * _softplus(x + dtb)
    pre, suf = g, g
    s = 1
    while s < c:
        pre = pre + jnp.where(row >= s, pltpu.roll(pre, s, 0), 0.0)
        suf = suf + jnp.where(row < c - s, pltpu.roll(suf, tt - s, 0), 0.0)
        s *= 2
    gc = jnp.where(lane < 3 * heads, pre, suf)
    return jnp.where(lane < 2 * heads, beta, gc)


def _gdn_intra_kernel(q_ref, k_ref, v_ref, ba_ref, alog_ref, dtb_ref, uf_ref, ub_ref, wqf_ref, wqb_ref, kdf_ref,
                      kdb_ref, at_ref, cd_ref, *, heads):
    c, hd = GDN_CHUNK, GDN_HEAD_DIM
    assert 2 * c == hd
    tt = q_ref.shape[0]
    ri = lax.broadcasted_iota(jnp.int32, (c, hd), 0)
    lane = lax.broadcasted_iota(jnp.int32, (c, hd), 1)
    left = lane < c
    cj = jnp.where(left, lane, lane - c)
    ahead = jnp.where(left, ri - cj, cj - ri)
    incl = ahead >= 0
    strict = ahead > 0
    eye2 = (ri == cj).astype(F32)
    gates = _gates(ba_ref[...], alog_ref[...], dtb_ref[...], heads)

    def blockdiag(pk):
        zero = jnp.zeros_like(pk)
        return jnp.concatenate([jnp.where(left, pk, zero), jnp.where(left, zero, pk)], axis=0)

    def rows(ck):
        return slice(ck * c, (ck + 1) * c)

    def cols(h):
        return slice(h * hd, (h + 1) * hd)

    groups = [(ck, h) for ck in range(tt // c) for h in range(heads)]
    ng = range(len(groups))
    g_ck = [gates[rows(ck), :] for ck in range(tt // c)]
    gt_ck = [jnp.concatenate([g, g], axis=0).T for g in g_ck]

    q = [q_ref[rows(ck), cols(h)].astype(F32) for ck, h in groups]
    k = [k_ref[rows(ck), cols(h)].astype(F32) for ck, h in groups]
    v = [v_ref[rows(ck), cols(h)].astype(F32) for ck, h in groups]
    col = lambda ck, l: g_ck[ck][:, l:l + 1]
    beta = [(col(ck, h), col(ck, heads + h)) for ck, h in groups]
    gc = [(col(ck, 2 * heads + h), col(ck, 3 * heads + h)) for ck, h in groups]
    beta2 = [jnp.where(left, beta[gi][0], beta[gi][1]) for gi in ng]
    gc2 = [jnp.where(left, gc[gi][0], gc[gi][1]) for gi in ng]
    grow2 = [jnp.where(left[0:1], gt_ck[ck][2 * heads + h:2 * heads + h + 1, :],
                       gt_ck[ck][3 * heads + h:3 * heads + h + 1, :]) for ck, h in groups]
    decay = [jnp.exp(jnp.where(incl, gc2[gi] - grow2[gi], -jnp.inf)) for gi in ng]
    gram = [_dot_nt(jnp.concatenate([q[gi], k[gi]], axis=0).astype(BF16),
                    jnp.concatenate([k[gi], k[gi]], axis=0).astype(BF16)) for gi in ng]
    attn = [(gram[gi][:c] * decay[gi]).astype(BF16) for gi in ng]
    p = [-jnp.where(strict, gram[gi][c:] * beta2[gi] * decay[gi], 0.0) for gi in ng]
    inv = [eye2 + p[gi] for gi in ng]
    n_sq = int(math.log2(c)) - 1
    pbd = [blockdiag(p[gi].astype(BF16)) for gi in ng]
    p = [_dot(p[gi].astype(BF16), pbd[gi]) for gi in ng]
    for _ in range(n_sq - 1):
        pbd = [blockdiag(p[gi].astype(BF16)) for gi in ng]
        both = [_dot(jnp.concatenate([p[gi], inv[gi]], axis=0).astype(BF16), pbd[gi]) for gi in ng]
        p = [both[gi][:c] for gi in ng]
        inv = [inv[gi] + both[gi][c:] for gi in ng]
    inv = [inv[gi] + _dot(inv[gi].astype(BF16), blockdiag(p[gi].astype(BF16))) for gi in ng]

    eg = [(jnp.exp(gc[gi][0]), jnp.exp(gc[gi][1])) for gi in ng]
    g_last = [(gc[gi][0][c - 1:c, :], gc[gi][1][0:1, :]) for gi in ng]
    rhs = [jnp.concatenate([jnp.concatenate([v[gi] * beta[gi][d], (k[gi] * beta[gi][d]) * eg[gi][d]], axis=1)
                            for d in range(2)], axis=0).astype(BF16) for gi in ng]
    uw = [_dot(blockdiag(inv[gi].astype(BF16)), rhs[gi]) for gi in ng]
    for gi, (ck, h) in enumerate(groups):
        at_ref[rows(ck), cols(h)] = attn[gi]
        for d, (u_ref, wq_ref, kd_ref) in enumerate(((uf_ref, wqf_ref, kdf_ref), (ub_ref, wqb_ref, kdb_ref))):
            u_ref[rows(ck), cols(h)] = uw[gi][d * c:(d + 1) * c, :hd]
            wq_ref[ck, :, cols(h)] = jnp.concatenate([uw[gi][d * c:(d + 1) * c, hd:], q[gi] * eg[gi][d]],
                                                     axis=0).astype(BF16)
            kd_ref[rows(ck), cols(h)] = (k[gi] * jnp.exp(g_last[gi][d] - gc[gi][d])).astype(BF16)
            cd_ref[ck, d * heads + h:d * heads + h + 1, :] = jnp.broadcast_to(jnp.exp(g_last[gi][d]), (1, hd))


def _gdn_intra_call(q, k, v, ba, alog_row, dtb_row, layer, tt):
    b, t, w = q.shape
    heads = w // GDN_HEAD_DIM
    c, hd = GDN_CHUNK, GDN_HEAD_DIM
    n, nck = t // c, tt // c
    tok = lambda width: pl.BlockSpec((None, tt, width), lambda bi, ti: (bi, ti, 0))
    lay = pl.BlockSpec((None, 1, ba.shape[-1]), lambda bi, ti: (layer, 0, 0))
    wq_spec = pl.BlockSpec((None, nck, 2 * c, w), lambda bi, ti: (bi, ti, 0, 0))
    cd_spec = pl.BlockSpec((None, nck, 2 * heads, hd), lambda bi, ti: (bi, ti, 0, 0))
    tok_sds = lambda dt: jax.ShapeDtypeStruct((b, t, w), dt)
    wq_sds = jax.ShapeDtypeStruct((b, n, 2 * c, w), BF16)
    return pl.pallas_call(
        functools.partial(_gdn_intra_kernel, heads=heads),
        grid=(b, t // tt),
        in_specs=[tok(w), tok(w), tok(w), tok(ba.shape[-1]), lay, lay],
        out_specs=[tok(w), tok(w), wq_spec, wq_spec, tok(w), tok(w), tok(w), cd_spec],
        out_shape=[tok_sds(F32), tok_sds(F32), wq_sds, wq_sds, tok_sds(BF16), tok_sds(BF16), tok_sds(BF16),
                   jax.ShapeDtypeStruct((b, n, 2 * heads, hd), F32)],
        compiler_params=_params("parallel", "parallel"),
        name="gdn_intra",
    )(q, k, v, ba, alog_row, dtb_row)


def _gdn_scan_kernel(uf, wqf, kdf, atf, cdf, ub, wqb, kdb, atb, cdb, s0_ref, of_ref, ob_ref, sfin_ref, s_scr, *,
                     heads, bb):
    i = pl.program_id(1)
    n = pl.num_programs(1)
    c, hd = GDN_CHUNK, GDN_HEAD_DIM

    @pl.when(i == 0)
    def _():
        s_scr[...] = s0_ref[...]

    dirs = ((uf, wqf, kdf, atf, cdf, of_ref), (ub, wqb, kdb, atb, cdb, ob_ref))
    chains = [(bi, d, h) for bi in range(bb) for d in range(2) for h in range(heads)]
    cols = lambda h: slice(h * hd, (h + 1) * hd)
    zeros = jnp.zeros((c, hd), BF16)
    s_prev = {ch: s_scr[ch] for ch in chains}
    ws = {(bi, d, h): _dot(dirs[d][1][bi, :, cols(h)], s_prev[bi, d, h].astype(BF16)) for bi, d, h in chains}
    v_new = {(bi, d, h): (dirs[d][0][bi, :, cols(h)] - ws[bi, d, h][:c]).astype(BF16) for bi, d, h in chains}
    v_pad = {(bi, d, h): jnp.concatenate([v_new[bi, d, h], zeros] if d == 0 else [zeros, v_new[bi, d, h]], axis=0)
             for bi, d, h in chains}
    o = {(bi, d, h): ws[bi, d, h][c:] + _dot(dirs[d][3][bi, :, cols(h)], v_pad[bi, d, h]) for bi, d, h in chains}
    for bi, d, h in chains:
        r = d * heads + h
        s_scr[bi, d, h] = (s_prev[bi, d, h] * dirs[d][4][bi, r:r + 1, :]
                           + _dot_tn(dirs[d][2][bi, :, cols(h)], v_new[bi, d, h]))
        dirs[d][5][bi, :, cols(h)] = o[bi, d, h].astype(dirs[d][5].dtype)

    @pl.when(i == n - 1)
    def _():
        sfin_ref[...] = s_scr[...]


def _gdn_scan_call(intra, s0):
    uf, ub, wqf, wqb, kdf, kdb, at, cd = intra
    b, t, w = uf.shape
    heads = w // GDN_HEAD_DIM
    c, hd = GDN_CHUNK, GDN_HEAD_DIM
    n = t // c
    bb = next(cand for cand in (4, 2, 1) if b % cand == 0)
    fwd = lambda bi, i: (bi, i, 0)
    bwd = lambda bi, i: (bi, n - 1 - i, 0)
    fwd4 = lambda bi, i: (bi, i, 0, 0)
    bwd4 = lambda bi, i: (bi, n - 1 - i, 0, 0)

    def specs(tok_idx, chunk_idx):
        return [pl.BlockSpec((bb, c, w), tok_idx), pl.BlockSpec((bb, None, 2 * c, w), chunk_idx),
                pl.BlockSpec((bb, c, w), tok_idx), pl.BlockSpec((bb, c, w), tok_idx),
                pl.BlockSpec((bb, None, 2 * heads, hd), chunk_idx)]

    sspec = pl.BlockSpec((bb, 2, heads, hd, hd), lambda bi, i: (bi, 0, 0, 0, 0))
    return pl.pallas_call(
        functools.partial(_gdn_scan_kernel, heads=heads, bb=bb),
        grid=(b // bb, n),
        in_specs=specs(fwd, fwd4) + specs(bwd, bwd4) + [sspec],
        out_specs=[pl.BlockSpec((bb, c, w), fwd), pl.BlockSpec((bb, c, w), bwd), sspec],
        out_shape=[jax.ShapeDtypeStruct((b, t, w), BF16), jax.ShapeDtypeStruct((b, t, w), BF16),
                   jax.ShapeDtypeStruct((b, 2, heads, hd, hd), F32)],
        scratch_shapes=[pltpu.VMEM((bb, 2, heads, hd, hd), F32)],
        compiler_params=_params("parallel", "arbitrary"),
        name="gdn_scan",
    )(uf, wqf, kdf, at, cd, ub, wqb, kdb, at, cd, s0)


def _post_kernel(na_ref, of_ref, ob_ref, z_ref, x_ref, mod_ref, nw_ref, wo_ref, g1_ref, b1_ref, w1_ref, w2_ref,
                 g2_ref, b2_ref, o_ref, *, alpha, n_split, n_sub):
    hd = GDN_HEAD_DIM
    na_w = na_ref.shape[-1]
    tm = x_ref.shape[0]
    sub = tm // n_sub
    groups = [slice(g * sub, (g + 1) * sub) for g in range(n_sub)]

    def gated(rs):
        o = of_ref[rs, :].astype(F32) + ob_ref[rs, :].astype(F32)
        z = z_ref[rs, :].astype(F32)
        parts = []
        for h in range(o.shape[-1] // hd):
            oh = o[:, h * hd:(h + 1) * hd]
            oh = oh * lax.rsqrt(jnp.mean(oh * oh, axis=-1, keepdims=True) + NORM_EPS) * nw_ref[...]
            parts.append(oh * _silu(z[:, h * hd:(h + 1) * hd]))
        return jnp.concatenate(parts, axis=1).astype(BF16)

    def mix(rs, gdn):
        y = _dot(na_ref[rs, :], wo_ref[0:na_w, :]) + _dot(gdn, wo_ref[na_w:, :])
        return _layer_norm(alpha * x_ref[rs, :] + mod_ref[2:3, :] * y, g1_ref[...], b1_ref[...])

    def mlp(x1):
        hb = (x1 * (1.0 + mod_ref[4:5, :]) + mod_ref[3:4, :]).astype(BF16)
        ck = w1_ref.shape[1] // n_split
        acc = None
        for j in range(n_split):
            a = jnp.maximum(_dot(hb, w1_ref[:, j * ck:(j + 1) * ck]), 0.0)
            part = _dot((a * a).astype(BF16), w2_ref[j * ck:(j + 1) * ck, :])
            acc = part if acc is None else acc + part
        return acc

    gdn = [gated(rs) for rs in groups]
    x1 = [mix(rs, gdn[g]) for g, rs in enumerate(groups)]
    acc = [mlp(x1[g]) for g in range(n_sub)]
    for g, rs in enumerate(groups):
        o_ref[rs, :] = _layer_norm(alpha * x1[g] + mod_ref[5:6, :] * acc[g], g2_ref[...], b2_ref[...])


def _post_call(na_out, o_f, o_b, z, x, mods, nw, w_out_b, ln1, w1_b, w2_b, ln2, layer, mod_row, tm, alpha):
    b, t, d = x.shape
    na_w, gw = na_out.shape[-1], o_f.shape[-1]
    dff = w1_b.shape[-1]
    mod_idx = (lambda bi, ti: (layer, bi, 0, 0)) if mod_row is None else (lambda bi, ti: (layer, mod_row, 0, 0))
    tok = lambda width: pl.BlockSpec((None, tm, width), lambda bi, ti: (bi, ti, 0))
    lay = lambda *shape: pl.BlockSpec((None,) + shape, lambda bi, ti: (layer,) + (0,) * len(shape),
                                      pipeline_mode=pl.Buffered(1))
    return pl.pallas_call(
        functools.partial(_post_kernel, alpha=alpha, n_split=4, n_sub=2 if tm >= 512 else 1),
        grid=(b, t // tm),
        in_specs=[tok(na_w), tok(gw), tok(gw), tok(gw), tok(d),
                  pl.BlockSpec((None, None, N_MOD, d), mod_idx),
                  lay(1, GDN_HEAD_DIM), lay(na_w + gw, d), lay(1, d), lay(1, d),
                  lay(d, dff), lay(dff, d), lay(1, d), lay(1, d)],
        out_specs=tok(d),
        out_shape=jax.ShapeDtypeStruct((b, t, d), F32),
        compiler_params=_params("parallel", "parallel"),
        name="post_mix_mlp",
    )(na_out, o_f, o_b, z, x, mods, nw, w_out_b, ln1[0], ln1[1], w1_b, w2_b, ln2[0], ln2[1])


def _token_tile(t, target):
    return target if t % target == 0 else t


def kernel(x, c, ctx, c_ctx, w_ada, b_ada, w_in, conv_w, a_log, dt_bias, gdn_norm_w, rpb, w_out, ln1_g, ln1_b,
           w_mlp1, w_mlp2, ln2_g, ln2_b):
    depth, d, _ = w_ada.shape
    b, t, _ = x.shape
    l = ctx.shape[1]
    gdn_heads = a_log.shape[-1]
    alpha = (2 * depth) ** 0.25
    assert b + 1 <= MOD_ROWS and t % GRID_W == 0 and t % GDN_CHUNK == 0 and l % GDN_CHUNK == 0

    cc = jnp.zeros((MOD_ROWS, d), F32).at[:b].set(c).at[b].set(c_ctx)
    mods = _ada_call(cc, w_ada, b_ada).reshape(depth, MOD_ROWS, N_MOD, d)
    w_in_b = jnp.pad(w_in, ((0, 0), (0, 0), (0, BA_PAD - 4 * gdn_heads))).astype(BF16)
    w_out_b = w_out.astype(BF16)
    w1_b = w_mlp1.astype(BF16)
    w2_b = w_mlp2.astype(BF16)
    gate_pad = ((0, 0), (0, 0), (2 * gdn_heads, BA_PAD - 4 * gdn_heads))
    alog_row = jnp.pad(a_log.reshape(depth, 1, 2 * gdn_heads), gate_pad)
    dtb_row = jnp.pad(dt_bias.reshape(depth, 1, 2 * gdn_heads), gate_pad)
    nw = gdn_norm_w.reshape(depth, 1, GDN_HEAD_DIM)
    ln1g, ln1b = ln1_g.reshape(depth, 1, d), ln1_b.reshape(depth, 1, d)
    ln2g, ln2b = ln2_g.reshape(depth, 1, d), ln2_b.reshape(depth, 1, d)
    rope_tabs = _rope_tables(t)
    win_r = min(NA_WIN_R, t // GRID_W)
    zeros_state = jnp.zeros((b, 2, gdn_heads, GDN_HEAD_DIM, GDN_HEAD_DIM), F32)

    tm_lat, tm_ctx = _token_tile(t, 512), _token_tile(l, 256)
    ti_lat, ti_ctx = _token_tile(t, 4 * GDN_CHUNK), _token_tile(l, 4 * GDN_CHUNK)
    x_lat, x_ctx = x, ctx
    for layer in range(depth):
        na_l, ql, kl, vl, z_l, ba_l = _inproj_call(x_lat, mods, w_in_b, conv_w, rope_tabs, layer, None, tm_lat)
        na_c, qc, kc, vc, z_c, ba_c = _inproj_call(x_ctx, mods, w_in_b, conv_w, None, layer, b, tm_ctx)

        na_out_l = _na_call(na_l, na_c, _na_bias_table(rpb[layer], win_r))

        of_c, ob_c, s_ctx = _gdn_scan_call(_gdn_intra_call(qc, kc, vc, ba_c, alog_row, dtb_row, layer, ti_ctx),
                                           zeros_state)
        of_l, ob_l, _ = _gdn_scan_call(_gdn_intra_call(ql, kl, vl, ba_l, alog_row, dtb_row, layer, ti_lat), s_ctx)

        x_lat = _post_call(na_out_l, of_l, ob_l, z_l, x_lat, mods, nw, w_out_b, (ln1g, ln1b), w1_b, w2_b,
                           (ln2g, ln2b), layer, None, tm_lat, alpha)
        if layer < depth - 1:
            na_out_c = _ctx_attn_call(na_c)
            x_ctx = _post_call(na_out_c, of_c, ob_c, z_c, x_ctx, mods, nw, w_out_b, (ln1g, ln1b), w1_b, w2_b,
                               (ln2g, ln2b), layer, b, tm_ctx, alpha)
    return x_lat
```

```python
import functools
import math

import jax
import jax.numpy as jnp
from jax import lax
from jax.experimental import pallas as pl
from jax.experimental.pallas import tpu as pltpu

GRID_W = 64
NA_HEAD_DIM = 64
NA_WIN_R = 8
NA_WIN_C = 16
GDN_HEAD_DIM = 128
GDN_CHUNK = 64
ROPE_BASE = 10000.0
N_MOD = 6
LN_EPS = 1e-5
NORM_EPS = 1e-6
MOD_ROWS = 16
BA_PAD = 128
VMEM_LIMIT = 56 * 1024 * 1024

BF16 = jnp.bfloat16
F32 = jnp.float32


def _params(*sem):
    return pltpu.CompilerParams(dimension_semantics=sem, vmem_limit_bytes=VMEM_LIMIT)


def _dot(a, b):
    return jnp.dot(a, b, preferred_element_type=F32)


def _dot_nt(a, b):
    return lax.dot_general(a, b, (((1,), (1,)), ((), ())), preferred_element_type=F32)


def _dot_tn(a, b):
    return lax.dot_general(a, b, (((0,), (0,)), ((), ())), preferred_element_type=F32)


def _silu(x):
    return x * jax.nn.sigmoid(x)


def _softplus(x):
    return jnp.maximum(x, 0.0) + jnp.log1p(jnp.exp(-jnp.abs(x)))


def _layer_norm(x, g, b):
    mu = jnp.mean(x, axis=-1, keepdims=True)
    xc = x - mu
    var = jnp.mean(xc * xc, axis=-1, keepdims=True)
    return xc * lax.rsqrt(var + LN_EPS) * g + b


def _ada_kernel(c_ref, w_ref, b_ref, o_ref):
    o_ref[...] = _dot(_silu(c_ref[...]), w_ref[...]) + b_ref[...]


def _ada_call(cc, w_ada, b_ada):
    depth, d, n = w_ada.shape
    tn = 1536
    return pl.pallas_call(
        _ada_kernel,
        grid=(depth, n // tn),
        in_specs=[pl.BlockSpec((MOD_ROWS, d), lambda l, j: (0, 0)),
                  pl.BlockSpec((None, d, tn), lambda l, j: (l, 0, j)),
                  pl.BlockSpec((None, 1, tn), lambda l, j: (l, 0, j))],
        out_specs=pl.BlockSpec((None, MOD_ROWS, tn), lambda l, j: (l, 0, j)),
        out_shape=jax.ShapeDtypeStruct((depth, MOD_ROWS, n), F32),
        compiler_params=_params("parallel", "parallel"),
        name="ada_mod",
    )(cc, w_ada, b_ada.reshape(depth, 1, n))


def _softmax_pv(s_list, v_list):
    m = None
    for s in s_list:
        sm = jnp.max(s, axis=-1, keepdims=True)
        m = sm if m is None else jnp.maximum(m, sm)
    den = None
    acc = None
    for s, v in zip(s_list, v_list):
        p = jnp.exp(s - m)
        ps = jnp.sum(p, axis=-1, keepdims=True)
        den = ps if den is None else den + ps
        pv = _dot(p.astype(BF16), v)
        acc = pv if acc is None else acc + pv
    return acc / den


def _na_kernel(q_ref, k_ref, v_ref, kc_ref, vc_ref, bias_ref, o_ref, *, heads, rows, win_r, rq):
    n_band = win_r * GRID_W
    scale = NA_HEAD_DIM ** -0.5
    pairs = range(heads // 2)
    gq = GRID_W
    lane = lax.broadcasted_iota(jnp.int32, (gq, 2 * NA_HEAD_DIM), 1)
    psl = lambda pr: slice(2 * pr * NA_HEAD_DIM, 2 * (pr + 1) * NA_HEAD_DIM)

    def scores(j):
        r = pl.program_id(1) * rq + j
        row_start = jnp.clip(r - win_r // 2, 0, rows - win_r)
        koff = pl.multiple_of(row_start * GRID_W, GRID_W)
        variant = row_start - r + win_r - 1
        s_lat, s_ctx = [], []
        for pr in pairs:
            qp = q_ref[j * gq:(j + 1) * gq, psl(pr)] * scale
            zero = jnp.zeros_like(qp)
            qs = jnp.concatenate([jnp.where(lane < NA_HEAD_DIM, qp, zero),
                                  jnp.where(lane >= NA_HEAD_DIM, qp, zero)], axis=0)
            bias = bias_ref[variant, 2 * pr:2 * pr + 2].reshape(2 * gq, n_band)
            s_lat.append(_dot_nt(qs, k_ref[pl.ds(koff, n_band), psl(pr)]) + bias)
            s_ctx.append(_dot_nt(qs, kc_ref[:, psl(pr)]))
        return koff, s_lat, s_ctx

    def softmax(st):
        koff, s_lat, s_ctx = st
        m = [jnp.maximum(jnp.max(s_lat[pr], axis=-1, keepdims=True), jnp.max(s_ctx[pr], axis=-1, keepdims=True))
             for pr in pairs]
        p_lat = [jnp.exp(s_lat[pr] - m[pr]) for pr in pairs]
        p_ctx = [jnp.exp(s_ctx[pr] - m[pr]) for pr in pairs]
        den = [jnp.sum(p_lat[pr], axis=-1, keepdims=True) + jnp.sum(p_ctx[pr], axis=-1, keepdims=True)
               for pr in pairs]
        return koff, [p.astype(BF16) for p in p_lat], [p.astype(BF16) for p in p_ctx], den

    def pv(j, st):
        koff, p_lat, p_ctx, den = st
        outs = []
        for pr in pairs:
            acc = (_dot(p_lat[pr], v_ref[pl.ds(koff, n_band), psl(pr)]) + _dot(p_ctx[pr], vc_ref[:, psl(pr)]))
            acc = acc / den[pr]
            outs.append(jnp.where(lane < NA_HEAD_DIM, acc[:gq], acc[gq:]))
        o_ref[j * gq:(j + 1) * gq, :] = jnp.concatenate(outs, axis=-1).astype(o_ref.dtype)

    pending = scores(0)
    for j in range(rq):
        nxt = scores(j + 1) if j + 1 < rq else None
        pv(j, softmax(pending))
        pending = nxt


def _na_bias_table(rpb, win_r):
    cols = jnp.arange(GRID_W)
    col_start = jnp.clip(cols - NA_WIN_C // 2, 0, GRID_W - NA_WIN_C)
    col_in = (cols[None, :] >= col_start[:, None]) & (cols[None, :] < col_start[:, None] + NA_WIN_C)
    edge = GRID_W - NA_WIN_C
    rpb_pad = jnp.pad(rpb.astype(F32), ((0, 0), (0, 0), (edge, edge)), mode="edge")
    rpb_cols = jnp.stack([rpb_pad[:, :, GRID_W - 1 - q:2 * GRID_W - 1 - q] for q in range(GRID_W)], axis=2)
    rpb_cols = jnp.where(col_in[None, None], rpb_cols, -jnp.inf)
    h = rpb.shape[0]
    variants = []
    for dv in range(win_r):
        dr0 = dv + NA_WIN_R - win_r
        band = rpb_cols[:, dr0:dr0 + win_r]
        variants.append(jnp.transpose(band, (0, 2, 1, 3)).reshape(h, GRID_W, win_r * GRID_W))
    return jnp.stack(variants)


def _na_call(na_lat, na_ctx, bias_tab):
    b, t, w3 = na_lat.shape
    w = w3 // 3
    heads = w // NA_HEAD_DIM
    l = na_ctx.shape[1]
    rows = t // GRID_W
    win_r = min(NA_WIN_R, rows)
    n_band = win_r * GRID_W
    rq = next(cand for cand in (4, 2, 1) if rows % cand == 0)
    return pl.pallas_call(
        functools.partial(_na_kernel, heads=heads, rows=rows, win_r=win_r, rq=rq),
        grid=(b, rows // rq),
        in_specs=[pl.BlockSpec((None, rq * GRID_W, w), lambda bi, r: (bi, r, 0)),
                  pl.BlockSpec((None, t, w), lambda bi, r: (bi, 0, 1)),
                  pl.BlockSpec((None, t, w), lambda bi, r: (bi, 0, 2)),
                  pl.BlockSpec((None, l, w), lambda bi, r: (bi, 0, 1)),
                  pl.BlockSpec((None, l, w), lambda bi, r: (bi, 0, 2)),
                  pl.BlockSpec((win_r, heads, GRID_W, n_band), lambda bi, r: (0, 0, 0, 0),
                               pipeline_mode=pl.Buffered(1))],
        out_specs=pl.BlockSpec((None, rq * GRID_W, w), lambda bi, r: (bi, r, 0)),
        out_shape=jax.ShapeDtypeStruct((b, t, w), BF16),
        compiler_params=_params("parallel", "arbitrary"),
        name="na_attn",
    )(na_lat, na_lat, na_lat, na_ctx, na_ctx, bias_tab)


def _ctx_attn_kernel(q_ref, k_ref, v_ref, o_ref, *, heads):
    scale = NA_HEAD_DIM ** -0.5
    outs = []
    for h in range(heads):
        sl = slice(h * NA_HEAD_DIM, (h + 1) * NA_HEAD_DIM)
        s = _dot_nt(q_ref[:, sl], k_ref[:, sl]) * scale
        outs.append(_softmax_pv([s], [v_ref[:, sl]]))
    o_ref[...] = jnp.concatenate(outs, axis=-1).astype(o_ref.dtype)


def _ctx_attn_call(na_ctx):
    b, l, w3 = na_ctx.shape
    w = w3 // 3
    return pl.pallas_call(
        functools.partial(_ctx_attn_kernel, heads=w // NA_HEAD_DIM),
        grid=(b,),
        in_specs=[pl.BlockSpec((None, l, w), lambda bi: (bi, 0, 0)),
                  pl.BlockSpec((None, l, w), lambda bi: (bi, 0, 1)),
                  pl.BlockSpec((None, l, w), lambda bi: (bi, 0, 2))],
        out_specs=pl.BlockSpec((None, l, w), lambda bi: (bi, 0, 0)),
        out_shape=jax.ShapeDtypeStruct((b, l, w), BF16),
        compiler_params=_params("parallel"),
        name="ctx_attn",
    )(na_ctx, na_ctx, na_ctx)


HALO = 16


def _l2norm(x):
    return x * lax.rsqrt(jnp.sum(x * x, axis=-1, keepdims=True) + NORM_EPS)


def _rope(x, cos, sin):
    quarter = GDN_HEAD_DIM // 4
    lane = lax.broadcasted_iota(jnp.int32, x.shape, 1)
    first = (lane % (2 * quarter)) < quarter
    partner = jnp.where(first, pltpu.roll(x, GDN_HEAD_DIM - quarter, 1), pltpu.roll(x, quarter, 1))
    return x * cos + partner * sin


def _inproj_kernel(*refs, na3, g3, zw, use_rope):
    xp_ref, x_ref, xn_ref, mod_ref, w_ref, cw_ref = refs[:6]
    if use_rope:
        cos_ref, sin_ref = refs[6:8]
    na_ref, q_ref, k_ref, v_ref, z_ref, ba_ref, g_scr = refs[-7:]
    ti = pl.program_id(1)
    nt = pl.num_programs(1)
    tm = x_ref.shape[0]
    hd = GDN_HEAD_DIM
    scale, shift = 1.0 + mod_ref[1:2, :], mod_ref[0:1, :]
    modulate = lambda ref: (ref[...] * scale + shift).astype(BF16)
    hb = modulate(x_ref)
    h_prev = jnp.where(ti > 0, modulate(xp_ref), jnp.zeros((HALO, x_ref.shape[1]), BF16))
    h_next = jnp.where(ti < nt - 1, modulate(xn_ref), jnp.zeros((HALO, x_ref.shape[1]), BF16))
    h_ext = jnp.concatenate([h_prev, hb, h_next], axis=0)

    kw = cw_ref.shape[0]
    half = kw // 2
    heads = g3 // 3 // hd
    outs = (q_ref, k_ref, v_ref)
    mxu_w = 2 * hd
    for slab in range(g3 // mxu_w):
        g_slab = _dot(h_ext, w_ref[:, na3 + slab * mxu_w:na3 + (slab + 1) * mxu_w])
        for sub in range(mxu_w // hd):
            blk = slab * (mxu_w // hd) + sub
            part, h = blk // heads, blk % heads
            cs = slice(blk * hd, (blk + 1) * hd)
            g_scr[blk] = g_slab[:, sub * hd:(sub + 1) * hd]
            acc = None
            for j in range(kw):
                term = g_scr[blk, HALO - half + j:HALO - half + j + tm, :] * cw_ref[j:j + 1, cs]
                acc = term if acc is None else acc + term
            y = _silu(acc)
            if part < 2:
                y = _l2norm(y)
                if use_rope:
                    y = _rope(y, cos_ref[...], sin_ref[...])
            if part == 0:
                y = y * (hd ** -0.5)
            outs[part][:, h * hd:(h + 1) * hd] = y.astype(outs[part].dtype)
    na_ref[...] = _dot(hb, w_ref[:, 0:na3]).astype(na_ref.dtype)
    z_ref[...] = _dot(hb, w_ref[:, na3 + g3:na3 + g3 + zw]).astype(z_ref.dtype)
    ba_ref[...] = _dot(hb, w_ref[:, na3 + g3 + zw:])


def _inproj_call(x, mods, w_in_b, conv_w, rope_tabs, layer, mod_row, tm):
    b, t, d = x.shape
    na3, g3, zw = 3 * (d // 2), 3 * (d // 2), d // 2
    ncols = w_in_b.shape[-1]
    kw = conv_w.shape[1]
    use_rope = rope_tabs is not None
    th, nh = tm // HALO, t // HALO
    mod_idx = (lambda bi, ti: (layer, bi, 0, 0)) if mod_row is None else (lambda bi, ti: (layer, mod_row, 0, 0))
    tok = lambda width: pl.BlockSpec((None, tm, width), lambda bi, ti: (bi, ti, 0))
    lay = lambda *shape: pl.BlockSpec((None,) + shape, lambda bi, ti: (layer,) + (0,) * len(shape),
                                      pipeline_mode=pl.Buffered(1))
    in_specs = [pl.BlockSpec((None, HALO, d), lambda bi, ti: (bi, jnp.maximum(ti * th - 1, 0), 0)),
                tok(d),
                pl.BlockSpec((None, HALO, d), lambda bi, ti: (bi, jnp.minimum((ti + 1) * th, nh - 1), 0)),
                pl.BlockSpec((None, None, N_MOD, d), mod_idx), lay(d, ncols), lay(kw, g3)]
    args = [x, x, x, mods, w_in_b, conv_w]
    if use_rope:
        in_specs += [pl.BlockSpec((tm, GDN_HEAD_DIM), lambda bi, ti: (ti, 0))] * 2
        args += list(rope_tabs)
    return pl.pallas_call(
        functools.partial(_inproj_kernel, na3=na3, g3=g3, zw=zw, use_rope=use_rope),
        grid=(b, t // tm),
        in_specs=in_specs,
        out_specs=[tok(na3), tok(g3 // 3), tok(g3 // 3), tok(g3 // 3), tok(zw), tok(BA_PAD)],
        out_shape=[jax.ShapeDtypeStruct((b, t, na3), BF16)] + [jax.ShapeDtypeStruct((b, t, g3 // 3), BF16)] * 3
        + [jax.ShapeDtypeStruct((b, t, zw), BF16), jax.ShapeDtypeStruct((b, t, BA_PAD), F32)],
        scratch_shapes=[pltpu.VMEM((g3 // GDN_HEAD_DIM, tm + 2 * HALO, GDN_HEAD_DIM), F32)],
        compiler_params=_params("parallel", "parallel"),
        name="in_proj_conv",
    )(*args)


def _rope_tables(t):
    pos = jnp.arange(t)
    row = (pos // GRID_W).astype(F32)
    col = (pos % GRID_W).astype(F32)
    axis_dim = GDN_HEAD_DIM // 2
    inv_freq = ROPE_BASE ** (-jnp.arange(0, axis_dim, 2, dtype=F32) / axis_dim)
    ang_r = row[:, None] * inv_freq[None, :]
    ang_c = col[:, None] * inv_freq[None, :]
    cos = jnp.concatenate([jnp.cos(ang_r), jnp.cos(ang_r), jnp.cos(ang_c), jnp.cos(ang_c)], axis=-1)
    sin = jnp.concatenate([-jnp.sin(ang_r), jnp.sin(ang_r), -jnp.sin(ang_c), jnp.sin(ang_c)], axis=-1)
    return cos, sin


def _gates(x, alog, dtb, heads):
    tt = x.shape[0]
    c = GDN_CHUNK
    lane = lax.broadcasted_iota(jnp.int32, x.shape, 1)
    row = lax.broadcasted_iota(jnp.int32, x.shape, 0) % c
    beta = jax.nn.sigmoid(x)
    g = -jnp.exp(alog) * _softplus(x + dtb)
    pre, suf = g, g
    s = 1
    while s < c:
        pre = pre + jnp.where(row >= s, pltpu.roll(pre, s, 0), 0.0)
        suf = suf + jnp.where(row < c - s, pltpu.roll(suf, tt - s, 0), 0.0)
        s *= 2
    gc = jnp.where(lane < 3 * heads, pre, suf)
    return jnp.where(lane < 2 * heads, beta, gc)


def _gdn_intra_kernel(q_ref, k_ref, v_ref, ba_ref, alog_ref, dtb_ref, uf_ref, ub_ref, wqf_ref, wqb_ref, kdf_ref,
                      kdb_ref, at_ref, cd_ref, *, heads):
    c, hd = GDN_CHUNK, GDN_HEAD_DIM
    assert 2 * c == hd
    tt = q_ref.shape[0]
    ri = lax.broadcasted_iota(jnp.int32, (c, hd), 0)
    lane = lax.broadcasted_iota(jnp.int32, (c, hd), 1)
    left = lane < c
    cj = jnp.where(left, lane, lane - c)
    ahead = jnp.where(left, ri - cj, cj - ri)
    incl = ahead >= 0
    strict = ahead > 0
    eye2 = (ri == cj).astype(F32)
    gates = _gates(ba_ref[...], alog_ref[...], dtb_ref[...], heads)

    def blockdiag(pk):
        zero = jnp.zeros_like(pk)
        return jnp.concatenate([jnp.where(left, pk, zero), jnp.where(left, zero, pk)], axis=0)

    def rows(ck):
        return slice(ck * c, (ck + 1) * c)

    def cols(h):
        return slice(h * hd, (h + 1) * hd)

    groups = [(ck, h) for ck in range(tt // c) for h in range(heads)]
    ng = range(len(groups))
    g_ck = [gates[rows(ck), :] for ck in range(tt // c)]
    gt_ck = [jnp.concatenate([g, g], axis=0).T for g in g_ck]

    q = [q_ref[rows(ck), cols(h)].astype(F32) for ck, h in groups]
    k = [k_ref[rows(ck), cols(h)].astype(F32) for ck, h in groups]
    v = [v_ref[rows(ck), cols(h)].astype(F32) for ck, h in groups]
    col = lambda ck, l: g_ck[ck][:, l:l + 1]
    beta = [(col(ck, h), col(ck, heads + h)) for ck, h in groups]
    gc = [(col(ck, 2 * heads + h), col(ck, 3 * heads + h)) for ck, h in groups]
    beta2 = [jnp.where(left, beta[gi][0], beta[gi][1]) for gi in ng]
    gc2 = [jnp.where(left, gc[gi][0], gc[gi][1]) for gi in ng]
    grow2 = [jnp.where(left[0:1], gt_ck[ck][2 * heads + h:2 * heads + h + 1, :],
                       gt_ck[ck][3 * heads + h:3 * heads + h + 1, :]) for ck, h in groups]
    decay = [jnp.exp(jnp.where(incl, gc2[gi] - grow2[gi], -jnp.inf)) for gi in ng]
    gram = [_dot_nt(jnp.concatenate([q[gi], k[gi]], axis=0).astype(BF16),
                    jnp.concatenate([k[gi], k[gi]], axis=0).astype(BF16)) for gi in ng]
    attn = [(gram[gi][:c] * decay[gi]).astype(BF16) for gi in ng]
    p = [-jnp.where(strict, gram[gi][c:] * beta2[gi] * decay[gi], 0.0) for gi in ng]
    inv = [eye2 + p[gi] for gi in ng]
    n_sq = int(math.log2(c)) - 1
    pbd = [blockdiag(p[gi].astype(BF16)) for gi in ng]
    p = [_dot(p[gi].astype(BF16), pbd[gi]) for gi in ng]
    for _ in range(n_sq - 1):
        pbd = [blockdiag(p[gi].astype(BF16)) for gi in ng]
        both = [_dot(jnp.concatenate([p[gi], inv[gi]], axis=0).astype(BF16), pbd[gi]) for gi in ng]
        p = [both[gi][:c] for gi in ng]
        inv = [inv[gi] + both[gi][c:] for gi in ng]
    inv = [inv[gi] + _dot(inv[gi].astype(BF16), blockdiag(p[gi].astype(BF16))) for gi in ng]

    eg = [(jnp.exp(gc[gi][0]), jnp.exp(gc[gi][1])) for gi in ng]
    g_last = [(gc[gi][0][c - 1:c, :], gc[gi][1][0:1, :]) for gi in ng]
    rhs = [jnp.concatenate([jnp.concatenate([v[gi] * beta[gi][d], (k[gi] * beta[gi][d]) * eg[gi][d]], axis=1)
                            for d in range(2)], axis=0).astype(BF16) for gi in ng]
    uw = [_dot(blockdiag(inv[gi].astype(BF16)), rhs[gi]) for gi in ng]
    for gi, (ck, h) in enumerate(groups):
        at_ref[rows(ck), cols(h)] = attn[gi]
        for d, (u_ref, wq_ref, kd_ref) in enumerate(((uf_ref, wqf_ref, kdf_ref), (ub_ref, wqb_ref, kdb_ref))):
            u_ref[rows(ck), cols(h)] = uw[gi][d * c:(d + 1) * c, :hd]
            wq_ref[ck, :, cols(h)] = jnp.concatenate([uw[gi][d * c:(d + 1) * c, hd:], q[gi] * eg[gi][d]],
                                                     axis=0).astype(BF16)
            kd_ref[rows(ck), cols(h)] = (k[gi] * jnp.exp(g_last[gi][d] - gc[gi][d])).astype(BF16)
            cd_ref[ck, d * heads + h:d * heads + h + 1, :] = jnp.broadcast_to(jnp.exp(g_last[gi][d]), (1, hd))


def _gdn_intra_call(q, k, v, ba, alog_row, dtb_row, layer, tt):
    b, t, w = q.shape
    heads = w // GDN_HEAD_DIM
    c, hd = GDN_CHUNK, GDN_HEAD_DIM
    n, nck = t // c, tt // c
    tok = lambda width: pl.BlockSpec((None, tt, width), lambda bi, ti: (bi, ti, 0))
    lay = pl.BlockSpec((None, 1, ba.shape[-1]), lambda bi, ti: (layer, 0, 0))
    wq_spec = pl.BlockSpec((None, nck, 2 * c, w), lambda bi, ti: (bi, ti, 0, 0))
    cd_spec = pl.BlockSpec((None, nck, 2 * heads, hd), lambda bi, ti: (bi, ti, 0, 0))
    tok_sds = lambda dt: jax.ShapeDtypeStruct((b, t, w), dt)
    wq_sds = jax.ShapeDtypeStruct((b, n, 2 * c, w), BF16)
    return pl.pallas_call(
        functools.partial(_gdn_intra_kernel, heads=heads),
        grid=(b, t // tt),
        in_specs=[tok(w), tok(w), tok(w), tok(ba.shape[-1]), lay, lay],
        out_specs=[tok(w), tok(w), wq_spec, wq_spec, tok(w), tok(w), tok(w), cd_spec],
        out_shape=[tok_sds(F32), tok_sds(F32), wq_sds, wq_sds, tok_sds(BF16), tok_sds(BF16), tok_sds(BF16),
                   jax.ShapeDtypeStruct((b, n, 2 * heads, hd), F32)],
        compiler_params=_params("parallel", "parallel"),
        name="gdn_intra",
    )(q, k, v, ba, alog_row, dtb_row)


def _gdn_scan_kernel(uf, wqf, kdf, atf, cdf, ub, wqb, kdb, atb, cdb, s0_ref, of_ref, ob_ref, sfin_ref, s_scr, *,
                     heads, bb):
    i = pl.program_id(1)
    n = pl.num_programs(1)
    c, hd = GDN_CHUNK, GDN_HEAD_DIM

    @pl.when(i == 0)
    def _():
        s_scr[...] = s0_ref[...]

    dirs = ((uf, wqf, kdf, atf, cdf, of_ref), (ub, wqb, kdb, atb, cdb, ob_ref))
    chains = [(bi, d, h) for bi in range(bb) for d in range(2) for h in range(heads)]
    cols = lambda h: slice(h * hd, (h + 1) * hd)
    zeros = jnp.zeros((c, hd), BF16)
    s_prev = {ch: s_scr[ch] for ch in chains}
    ws = {(bi, d, h): _dot(dirs[d][1][bi, :, cols(h)], s_prev[bi, d, h].astype(BF16)) for bi, d, h in chains}
    v_new = {(bi, d, h): (dirs[d][0][bi, :, cols(h)] - ws[bi, d, h][:c]).astype(BF16) for bi, d, h in chains}
    v_pad = {(bi, d, h): jnp.concatenate([v_new[bi, d, h], zeros] if d == 0 else [zeros, v_new[bi, d, h]], axis=0)
             for bi, d, h in chains}
    o = {(bi, d, h): ws[bi, d, h][c:] + _dot(dirs[d][3][bi, :, cols(h)], v_pad[bi, d, h]) for bi, d, h in chains}
    for bi, d, h in chains:
        r = d * heads + h
        s_scr[bi, d, h] = (s_prev[bi, d, h] * dirs[d][4][bi, r:r + 1, :]
                           + _dot_tn(dirs[d][2][bi, :, cols(h)], v_new[bi, d, h]))
        dirs[d][5][bi, :, cols(h)] = o[bi, d, h].astype(dirs[d][5].dtype)

    @pl.when(i == n - 1)
    def _():
        sfin_ref[...] = s_scr[...]


def _gdn_scan_call(intra, s0):
    uf, ub, wqf, wqb, kdf, kdb, at, cd = intra
    b, t, w = uf.shape
    heads = w // GDN_HEAD_DIM
    c, hd = GDN_CHUNK, GDN_HEAD_DIM
    n = t // c
    bb = next(cand for cand in (4, 2, 1) if b % cand == 0)
    fwd = lambda bi, i: (bi, i, 0)
    bwd = lambda bi, i: (bi, n - 1 - i, 0)
    fwd4 = lambda bi, i: (bi, i, 0, 0)
    bwd4 = lambda bi, i: (bi, n - 1 - i, 0, 0)

    def specs(tok_idx, chunk_idx):
        return [pl.BlockSpec((bb, c, w), tok_idx), pl.BlockSpec((bb, None, 2 * c, w), chunk_idx),
                pl.BlockSpec((bb, c, w), tok_idx), pl.BlockSpec((bb, c, w), tok_idx),
                pl.BlockSpec((bb, None, 2 * heads, hd), chunk_idx)]

    sspec = pl.BlockSpec((bb, 2, heads, hd, hd), lambda bi, i: (bi, 0, 0, 0, 0))
    return pl.pallas_call(
        functools.partial(_gdn_scan_kernel, heads=heads, bb=bb),
        grid=(b // bb, n),
        in_specs=specs(fwd, fwd4) + specs(bwd, bwd4) + [sspec],
        out_specs=[pl.BlockSpec((bb, c, w), fwd), pl.BlockSpec((bb, c, w), bwd), sspec],
        out_shape=[jax.ShapeDtypeStruct((b, t, w), BF16), jax.ShapeDtypeStruct((b, t, w), BF16),
                   jax.ShapeDtypeStruct((b, 2, heads, hd, hd), F32)],
        scratch_shapes=[pltpu.VMEM((bb, 2, heads, hd, hd), F32)],
        compiler_params=_params("parallel", "arbitrary"),
        name="gdn_scan",
    )(uf, wqf, kdf, at, cd, ub, wqb, kdb, at, cd, s0)


def _post_kernel(na_ref, of_ref, ob_ref, z_ref, x_ref, mod_ref, nw_ref, wo_ref, g1_ref, b1_ref, w1_ref, w2_ref,
                 g2_ref, b2_ref, o_ref, *, alpha, n_split, n_sub):
    hd = GDN_HEAD_DIM
    na_w = na_ref.shape[-1]
    tm = x_ref.shape[0]
    sub = tm // n_sub
    groups = [slice(g * sub, (g + 1) * sub) for g in range(n_sub)]

    def gated(rs):
        o = of_ref[rs, :].astype(F32) + ob_ref[rs, :].astype(F32)
        z = z_ref[rs, :].astype(F32)
        parts = []
        for h in range(o.shape[-1] // hd):
            oh = o[:, h * hd:(h + 1) * hd]
            oh = oh * lax.rsqrt(jnp.mean(oh * oh, axis=-1, keepdims=True) + NORM_EPS) * nw_ref[...]
            parts.append(oh * _silu(z[:, h * hd:(h + 1) * hd]))
        return jnp.concatenate(parts, axis=1).astype(BF16)

    def mix(rs, gdn):
        y = _dot(na_ref[rs, :], wo_ref[0:na_w, :]) + _dot(gdn, wo_ref[na_w:, :])
        return _layer_norm(alpha * x_ref[rs, :] + mod_ref[2:3, :] * y, g1_ref[...], b1_ref[...])

    def mlp(x1):
        hb = (x1 * (1.0 + mod_ref[4:5, :]) + mod_ref[3:4, :]).astype(BF16)
        ck = w1_ref.shape[1] // n_split
        acc = None
        for j in range(n_split):
            a = jnp.maximum(_dot(hb, w1_ref[:, j * ck:(j + 1) * ck]), 0.0)
            part = _dot((a * a).astype(BF16), w2_ref[j * ck:(j + 1) * ck, :])
            acc = part if acc is None else acc + part
        return acc

    gdn = [gated(rs) for rs in groups]
    x1 = [mix(rs, gdn[g]) for g, rs in enumerate(groups)]
    acc = [mlp(x1[g]) for g in range(n_sub)]
    for g, rs in enumerate(groups):
        o_ref[rs, :] = _layer_norm(alpha * x1[g] + mod_ref[5:6, :] * acc[g], g2_ref[...], b2_ref[...])


def _post_call(na_out, o_f, o_b, z, x, mods, nw, w_out_b, ln1, w1_b, w2_b, ln2, layer, mod_row, tm, alpha):
    b, t, d = x.shape
    na_w, gw = na_out.shape[-1], o_f.shape[-1]
    dff = w1_b.shape[-1]
    mod_idx = (lambda bi, ti: (layer, bi, 0, 0)) if mod_row is None else (lambda bi, ti: (layer, mod_row, 0, 0))
    tok = lambda width: pl.BlockSpec((None, tm, width), lambda bi, ti: (bi, ti, 0))
    lay = lambda *shape: pl.BlockSpec((None,) + shape, lambda bi, ti: (layer,) + (0,) * len(shape),
                                      pipeline_mode=pl.Buffered(1))
    return pl.pallas_call(
        functools.partial(_post_kernel, alpha=alpha, n_split=4, n_sub=2 if tm >= 512 else 1),
        grid=(b, t // tm),
        in_specs=[tok(na_w), tok(gw), tok(gw), tok(gw), tok(d),
                  pl.BlockSpec((None, None, N_MOD, d), mod_idx),
                  lay(1, GDN_HEAD_DIM), lay(na_w + gw, d), lay(1, d), lay(1, d),
                  lay(d, dff), lay(dff, d), lay(1, d), lay(1, d)],
        out_specs=tok(d),
        out_shape=jax.ShapeDtypeStruct((b, t, d), F32),
        compiler_params=_params("parallel", "parallel"),
        name="post_mix_mlp",
    )(na_out, o_f, o_b, z, x, mods, nw, w_out_b, ln1[0], ln1[1], w1_b, w2_b, ln2[0], ln2[1])


def _token_tile(t, target):
    return target if t % target == 0 else t


def kernel(x, c, ctx, c_ctx, w_ada, b_ada, w_in, conv_w, a_log, dt_bias, gdn_norm_w, rpb, w_out, ln1_g, ln1_b,
           w_mlp1, w_mlp2, ln2_g, ln2_b):
    depth, d, _ = w_ada.shape
    b, t, _ = x.shape
    l = ctx.shape[1]
    gdn_heads = a_log.shape[-1]
    alpha = (2 * depth) ** 0.25
    assert b + 1 <= MOD_ROWS and t % GRID_W == 0 and t % GDN_CHUNK == 0 and l % GDN_CHUNK == 0

    cc = jnp.zeros((MOD_ROWS, d), F32).at[:b].set(c).at[b].set(c_ctx)
    mods = _ada_call(cc, w_ada, b_ada).reshape(depth, MOD_ROWS, N_MOD, d)
    w_in_b = jnp.pad(w_in, ((0, 0), (0, 0), (0, BA_PAD - 4 * gdn_heads))).astype(BF16)
    w_out_b = w_out.astype(BF16)
    w1_b = w_mlp1.astype(BF16)
    w2_b = w_mlp2.astype(BF16)
    gate_pad = ((0, 0), (0, 0), (2 * gdn_heads, BA_PAD - 4 * gdn_heads))
    alog_row = jnp.pad(a_log.reshape(depth, 1, 2 * gdn_heads), gate_pad)
    dtb_row = jnp.pad(dt_bias.reshape(depth, 1, 2 * gdn_heads), gate_pad)
    nw = gdn_norm_w.reshape(depth, 1, GDN_HEAD_DIM)
    ln1g, ln1b = ln1_g.reshape(depth, 1, d), ln1_b.reshape(depth, 1, d)
    ln2g, ln2b = ln2_g.reshape(depth, 1, d), ln2_b.reshape(depth, 1, d)
    rope_tabs = _rope_tables(t)
    win_r = min(NA_WIN_R, t // GRID_W)
    zeros_state = jnp.zeros((b, 2, gdn_heads, GDN_HEAD_DIM, GDN_HEAD_DIM), F32)

    tm_lat, tm_ctx = _token_tile(t, 512), _token_tile(l, 256)
    ti_lat, ti_ctx = _token_tile(t, 4 * GDN_CHUNK), _token_tile(l, 4 * GDN_CHUNK)
    x_lat, x_ctx = x, ctx
    for layer in range(depth):
        na_l, ql, kl, vl, z_l, ba_l = _inproj_call(x_lat, mods, w_in_b, conv_w, rope_tabs, layer, None, tm_lat)
        na_c, qc, kc, vc, z_c, ba_c = _inproj_call(x_ctx, mods, w_in_b, conv_w, None, layer, b, tm_ctx)

        na_out_l = _na_call(na_l, na_c, _na_bias_table(rpb[layer], win_r))

        of_c, ob_c, s_ctx = _gdn_scan_call(_gdn_intra_call(qc, kc, vc, ba_c, alog_row, dtb_row, layer, ti_ctx),
                                           zeros_state)
        of_l, ob_l, _ = _gdn_scan_call(_gdn_intra_call(ql, kl, vl, ba_l, alog_row, dtb_row, layer, ti_lat), s_ctx)

        x_lat = _post_call(na_out_l, of_l, ob_l, z_l, x_lat, mods, nw, w_out_b, (ln1g, ln1b), w1_b, w2_b,
                           (ln2g, ln2b), layer, None, tm_lat, alpha)
        if layer < depth - 1:
            na_out_c = _ctx_attn_call(na_c)
            x_ctx = _post_call(na_out_c, of_c, ob_c, z_c, x_ctx, mods, nw, w_out_b, (ln1g, ln1b), w1_b, w2_b,
                               (ln2g, ln2b), layer, b, tm_ctx, alpha)
    return x_lat
```

```python
import functools
import math

import jax
import jax.numpy as jnp
from jax import lax
from jax.experimental import pallas as pl
from jax.experimental.pallas import tpu as pltpu

GRID_W = 64
NA_HEAD_DIM = 64
NA_WIN_R = 8
NA_WIN_C = 16
GDN_HEAD_DIM = 128
GDN_CHUNK = 64
ROPE_BASE = 10000.0
N_MOD = 6
LN_EPS = 1e-5
NORM_EPS = 1e-6
MOD_ROWS = 16
BA_PAD = 128
VMEM_LIMIT = 56 * 1024 * 1024
TM_IN_PROJ = 1024
TM_POST = 512
TM_CTX = 256
TT_INTRA = 4 * GDN_CHUNK
TN_ADA = 1536

BF16 = jnp.bfloat16
F32 = jnp.float32


def _params(*sem):
    return pltpu.CompilerParams(dimension_semantics=sem, vmem_limit_bytes=VMEM_LIMIT)


def _dot(a, b):
    return jnp.dot(a, b, preferred_element_type=F32)


def _dot_nt(a, b):
    return lax.dot_general(a, b, (((1,), (1,)), ((), ())), preferred_element_type=F32)


def _dot_tn(a, b):
    return lax.dot_general(a, b, (((0,), (0,)), ((), ())), preferred_element_type=F32)


def _silu(x):
    return x * jax.nn.sigmoid(x)


def _softplus(x):
    return jnp.maximum(x, 0.0) + jnp.log1p(jnp.exp(-jnp.abs(x)))


def _layer_norm(x, g, b):
    mu = jnp.mean(x, axis=-1, keepdims=True)
    xc = x - mu
    var = jnp.mean(xc * xc, axis=-1, keepdims=True)
    return xc * lax.rsqrt(var + LN_EPS) * g + b


def _ada_kernel(c_ref, w_ref, b_ref, o_ref):
    o_ref[...] = _dot(_silu(c_ref[...]), w_ref[...]) + b_ref[...]


def _ada_call(cc, w_ada, b_ada):
    depth, d, n = w_ada.shape
    tn = TN_ADA if n % TN_ADA == 0 else n
    return pl.pallas_call(
        _ada_kernel,
        grid=(depth, n // tn),
        in_specs=[pl.BlockSpec((MOD_ROWS, d), lambda l, j: (0, 0)),
                  pl.BlockSpec((None, d, tn), lambda l, j: (l, 0, j)),
                  pl.BlockSpec((None, 1, tn), lambda l, j: (l, 0, j))],
        out_specs=pl.BlockSpec((None, MOD_ROWS, tn), lambda l, j: (l, 0, j)),
        out_shape=jax.ShapeDtypeStruct((depth, MOD_ROWS, n), F32),
        compiler_params=_params("parallel", "parallel"),
        name="ada_mod",
    )(cc, w_ada, b_ada.reshape(depth, 1, n))


def _softmax_pv(s_list, v_list):
    m = None
    for s in s_list:
        sm = jnp.max(s, axis=-1, keepdims=True)
        m = sm if m is None else jnp.maximum(m, sm)
    den = None
    acc = None
    for s, v in zip(s_list, v_list):
        p = jnp.exp(s - m)
        ps = jnp.sum(p, axis=-1, keepdims=True)
        den = ps if den is None else den + ps
        pv = _dot(p.astype(BF16), v)
        acc = pv if acc is None else acc + pv
    return acc / den


def _na_kernel(q_ref, k_ref, v_ref, kc_ref, vc_ref, bias_ref, o_ref, *, heads, rows, win_r, rq):
    n_band = win_r * GRID_W
    scale = NA_HEAD_DIM ** -0.5
    pairs = range(heads // 2)
    gq = GRID_W
    lane = lax.broadcasted_iota(jnp.int32, (gq, 2 * NA_HEAD_DIM), 1)
    psl = lambda pr: slice(2 * pr * NA_HEAD_DIM, 2 * (pr + 1) * NA_HEAD_DIM)

    def scores(j):
        r = pl.program_id(1) * rq + j
        row_start = jnp.clip(r - win_r // 2, 0, rows - win_r)
        koff = pl.multiple_of(row_start * GRID_W, GRID_W)
        variant = row_start - r + win_r - 1
        s_lat, s_ctx = [], []
        for pr in pairs:
            qp = q_ref[j * gq:(j + 1) * gq, psl(pr)] * scale
            zero = jnp.zeros_like(qp)
            qs = jnp.concatenate([jnp.where(lane < NA_HEAD_DIM, qp, zero),
                                  jnp.where(lane >= NA_HEAD_DIM, qp, zero)], axis=0)
            bias = bias_ref[variant, 2 * pr:2 * pr + 2].reshape(2 * gq, n_band)
            s_lat.append(_dot_nt(qs, k_ref[pl.ds(koff, n_band), psl(pr)]) + bias)
            s_ctx.append(_dot_nt(qs, kc_ref[:, psl(pr)]))
        return koff, s_lat, s_ctx

    def softmax(st):
        koff, s_lat, s_ctx = st
        m = [jnp.maximum(jnp.max(s_lat[pr], axis=-1, keepdims=True), jnp.max(s_ctx[pr], axis=-1, keepdims=True))
             for pr in pairs]
        p_lat = [jnp.exp(s_lat[pr] - m[pr]) for pr in pairs]
        p_ctx = [jnp.exp(s_ctx[pr] - m[pr]) for pr in pairs]
        den = [jnp.sum(p_lat[pr], axis=-1, keepdims=True) + jnp.sum(p_ctx[pr], axis=-1, keepdims=True)
               for pr in pairs]
        return koff, [p.astype(BF16) for p in p_lat], [p.astype(BF16) for p in p_ctx], den

    def pv(j, st):
        koff, p_lat, p_ctx, den = st
        outs = []
        for pr in pairs:
            acc = (_dot(p_lat[pr], v_ref[pl.ds(koff, n_band), psl(pr)]) + _dot(p_ctx[pr], vc_ref[:, psl(pr)]))
            acc = acc / den[pr]
            outs.append(jnp.where(lane < NA_HEAD_DIM, acc[:gq], acc[gq:]))
        o_ref[j * gq:(j + 1) * gq, :] = jnp.concatenate(outs, axis=-1).astype(o_ref.dtype)

    pending = scores(0)
    for j in range(rq):
        nxt = scores(j + 1) if j + 1 < rq else None
        pv(j, softmax(pending))
        pending = nxt


def _na_bias_table(rpb, win_r):
    cols = jnp.arange(GRID_W)
    col_start = jnp.clip(cols - NA_WIN_C // 2, 0, GRID_W - NA_WIN_C)
    col_in = (cols[None, :] >= col_start[:, None]) & (cols[None, :] < col_start[:, None] + NA_WIN_C)
    edge = GRID_W - NA_WIN_C
    rpb_pad = jnp.pad(rpb.astype(F32), ((0, 0), (0, 0), (edge, edge)), mode="edge")
    rpb_cols = jnp.stack([rpb_pad[:, :, GRID_W - 1 - q:2 * GRID_W - 1 - q] for q in range(GRID_W)], axis=2)
    rpb_cols = jnp.where(col_in[None, None], rpb_cols, -jnp.inf)
    h = rpb.shape[0]
    variants = []
    for dv in range(win_r):
        dr0 = dv + NA_WIN_R - win_r
        band = rpb_cols[:, dr0:dr0 + win_r]
        variants.append(jnp.transpose(band, (0, 2, 1, 3)).reshape(h, GRID_W, win_r * GRID_W))
    return jnp.stack(variants)


def _na_call(na_lat, na_ctx, bias_tab):
    b, t, w3 = na_lat.shape
    w = w3 // 3
    heads = w // NA_HEAD_DIM
    l = na_ctx.shape[1]
    rows = t // GRID_W
    win_r = min(NA_WIN_R, rows)
    n_band = win_r * GRID_W
    rq = next(cand for cand in (4, 2, 1) if rows % cand == 0)
    return pl.pallas_call(
        functools.partial(_na_kernel, heads=heads, rows=rows, win_r=win_r, rq=rq),
        grid=(b, rows // rq),
        in_specs=[pl.BlockSpec((None, rq * GRID_W, w), lambda bi, r: (bi, r, 0)),
                  pl.BlockSpec((None, t, w), lambda bi, r: (bi, 0, 1)),
                  pl.BlockSpec((None, t, w), lambda bi, r: (bi, 0, 2)),
                  pl.BlockSpec((None, l, w), lambda bi, r: (bi, 0, 1)),
                  pl.BlockSpec((None, l, w), lambda bi, r: (bi, 0, 2)),
                  pl.BlockSpec((win_r, heads, GRID_W, n_band), lambda bi, r: (0, 0, 0, 0),
                               pipeline_mode=pl.Buffered(1))],
        out_specs=pl.BlockSpec((None, rq * GRID_W, w), lambda bi, r: (bi, r, 0)),
        out_shape=jax.ShapeDtypeStruct((b, t, w), BF16),
        compiler_params=_params("parallel", "arbitrary"),
        name="na_attn",
    )(na_lat, na_lat, na_lat, na_ctx, na_ctx, bias_tab)


def _ctx_attn_kernel(q_ref, k_ref, v_ref, o_ref, *, heads):
    scale = NA_HEAD_DIM ** -0.5
    outs = []
    for h in range(heads):
        sl = slice(h * NA_HEAD_DIM, (h + 1) * NA_HEAD_DIM)
        s = _dot_nt(q_ref[:, sl], k_ref[:, sl]) * scale
        outs.append(_softmax_pv([s], [v_ref[:, sl]]))
    o_ref[...] = jnp.concatenate(outs, axis=-1).astype(o_ref.dtype)


def _ctx_attn_call(na_ctx):
    b, l, w3 = na_ctx.shape
    w = w3 // 3
    return pl.pallas_call(
        functools.partial(_ctx_attn_kernel, heads=w // NA_HEAD_DIM),
        grid=(b,),
        in_specs=[pl.BlockSpec((None, l, w), lambda bi: (bi, 0, 0)),
                  pl.BlockSpec((None, l, w), lambda bi: (bi, 0, 1)),
                  pl.BlockSpec((None, l, w), lambda bi: (bi, 0, 2))],
        out_specs=pl.BlockSpec((None, l, w), lambda bi: (bi, 0, 0)),
        out_shape=jax.ShapeDtypeStruct((b, l, w), BF16),
        compiler_params=_params("parallel"),
        name="ctx_attn",
    )(na_ctx, na_ctx, na_ctx)


HALO = 16


def _l2norm(x):
    return x * lax.rsqrt(jnp.sum(x * x, axis=-1, keepdims=True) + NORM_EPS)


def _rope(x, cos, sin):
    quarter = GDN_HEAD_DIM // 4
    lane = lax.broadcasted_iota(jnp.int32, x.shape, 1)
    first = (lane % (2 * quarter)) < quarter
    partner = jnp.where(first, pltpu.roll(x, GDN_HEAD_DIM - quarter, 1), pltpu.roll(x, quarter, 1))
    return x * cos + partner * sin


def _inproj_kernel(*refs, na3, g3, zw, use_rope):
    xp_ref, x_ref, xn_ref, mod_ref, w_ref, cw_ref = refs[:6]
    if use_rope:
        cos_ref, sin_ref = refs[6:8]
    na_ref, q_ref, k_ref, v_ref, z_ref, ba_ref, g_scr = refs[-7:]
    ti = pl.program_id(1)
    nt = pl.num_programs(1)
    tm = x_ref.shape[0]
    hd = GDN_HEAD_DIM
    scale, shift = 1.0 + mod_ref[1:2, :], mod_ref[0:1, :]
    modulate = lambda ref: (ref[...] * scale + shift).astype(BF16)
    hb = modulate(x_ref)
    h_prev = jnp.where(ti > 0, modulate(xp_ref), jnp.zeros((HALO, x_ref.shape[1]), BF16))
    h_next = jnp.where(ti < nt - 1, modulate(xn_ref), jnp.zeros((HALO, x_ref.shape[1]), BF16))
    h_ext = jnp.concatenate([h_prev, hb, h_next], axis=0)

    kw = cw_ref.shape[0]
    half = kw // 2
    heads = g3 // 3 // hd
    outs = (q_ref, k_ref, v_ref)
    mxu_w = 2 * hd
    for slab in range(g3 // mxu_w):
        g_slab = _dot(h_ext, w_ref[:, na3 + slab * mxu_w:na3 + (slab + 1) * mxu_w])
        for sub in range(mxu_w // hd):
            blk = slab * (mxu_w // hd) + sub
            part, h = blk // heads, blk % heads
            cs = slice(blk * hd, (blk + 1) * hd)
            g_scr[blk] = g_slab[:, sub * hd:(sub + 1) * hd]
            acc = None
            for j in range(kw):
                term = g_scr[blk, HALO - half + j:HALO - half + j + tm, :] * cw_ref[j:j + 1, cs]
                acc = term if acc is None else acc + term
            y = _silu(acc)
            if part < 2:
                y = _l2norm(y)
                if use_rope:
                    y = _rope(y, cos_ref[...], sin_ref[...])
            if part == 0:
                y = y * (hd ** -0.5)
            outs[part][:, h * hd:(h + 1) * hd] = y.astype(outs[part].dtype)
    na_ref[...] = _dot(hb, w_ref[:, 0:na3]).astype(na_ref.dtype)
    z_ref[...] = _dot(hb, w_ref[:, na3 + g3:na3 + g3 + zw]).astype(z_ref.dtype)
    ba_ref[...] = _dot(hb, w_ref[:, na3 + g3 + zw:])


def _inproj_call(x, mods, w_in_b, conv_w, rope_tabs, layer, mod_row, tm):
    b, t, d = x.shape
    na3, g3, zw = 3 * (d // 2), 3 * (d // 2), d // 2
    ncols = w_in_b.shape[-1]
    kw = conv_w.shape[1]
    use_rope = rope_tabs is not None
    th, nh = tm // HALO, t // HALO
    mod_idx = (lambda bi, ti: (layer, bi, 0, 0)) if mod_row is None else (lambda bi, ti: (layer, mod_row, 0, 0))
    tok = lambda width: pl.BlockSpec((None, tm, width), lambda bi, ti: (bi, ti, 0))
    lay = lambda *shape: pl.BlockSpec((None,) + shape, lambda bi, ti: (layer,) + (0,) * len(shape),
                                      pipeline_mode=pl.Buffered(1))
    in_specs = [pl.BlockSpec((None, HALO, d), lambda bi, ti: (bi, jnp.maximum(ti * th - 1, 0), 0)),
                tok(d),
                pl.BlockSpec((None, HALO, d), lambda bi, ti: (bi, jnp.minimum((ti + 1) * th, nh - 1), 0)),
                pl.BlockSpec((None, None, N_MOD, d), mod_idx), lay(d, ncols), lay(kw, g3)]
    args = [x, x, x, mods, w_in_b, conv_w]
    if use_rope:
        in_specs += [pl.BlockSpec((tm, GDN_HEAD_DIM), lambda bi, ti: (ti, 0))] * 2
        args += list(rope_tabs)
    return pl.pallas_call(
        functools.partial(_inproj_kernel, na3=na3, g3=g3, zw=zw, use_rope=use_rope),
        grid=(b, t // tm),
        in_specs=in_specs,
        out_specs=[tok(na3), tok(g3 // 3), tok(g3 // 3), tok(g3 // 3), tok(zw), tok(BA_PAD)],
        out_shape=[jax.ShapeDtypeStruct((b, t, na3), BF16)] + [jax.ShapeDtypeStruct((b, t, g3 // 3), BF16)] * 3
        + [jax.ShapeDtypeStruct((b, t, zw), BF16), jax.ShapeDtypeStruct((b, t, BA_PAD), F32)],
        scratch_shapes=[pltpu.VMEM((g3 // GDN_HEAD_DIM, tm + 2 * HALO, GDN_HEAD_DIM), F32)],
        compiler_params=_params("parallel", "parallel"),
        name="in_proj_conv",
    )(*args)


def _rope_tables(t):
    pos = jnp.arange(t)
    row = (pos // GRID_W).astype(F32)
    col = (pos % GRID_W).astype(F32)
    axis_dim = GDN_HEAD_DIM // 2
    inv_freq = ROPE_BASE ** (-jnp.arange(0, axis_dim, 2, dtype=F32) / axis_dim)
    ang_r = row[:, None] * inv_freq[None, :]
    ang_c = col[:, None] * inv_freq[None, :]
    cos = jnp.concatenate([jnp.cos(ang_r), jnp.cos(ang_r), jnp.cos(ang_c), jnp.cos(ang_c)], axis=-1)
    sin = jnp.concatenate([-jnp.sin(ang_r), jnp.sin(ang_r), -jnp.sin(ang_c), jnp.sin(ang_c)], axis=-1)
    return cos, sin


def _gates(x, alog, dtb, heads):
    tt = x.shape[0]
    c = GDN_CHUNK
    lane = lax.broadcasted_iota(jnp.int32, x.shape, 1)
    row = lax.broadcasted_iota(jnp.int32, x.shape, 0) % c
    beta = jax.nn.sigmoid(x)
    g = -jnp.exp(alog) * _softplus(x + dtb)
    pre, suf = g, g
    s = 1
    while s < c:
        pre = pre + jnp.where(row >= s, pltpu.roll(pre, s, 0), 0.0)
        suf = suf + jnp.where(row < c - s, pltpu.roll(suf, tt - s, 0), 0.0)
        s *= 2
    gc = jnp.where(lane < 3 * heads, pre, suf)
    return jnp.where(lane < 2 * heads, beta, gc)


def _gdn_intra_kernel(q_ref, k_ref, v_ref, ba_ref, alog_ref, dtb_ref, uf_ref, ub_ref, wqf_ref, wqb_ref, kdf_ref,
                      kdb_ref, at_ref, cd_ref, *, heads):
    c, hd = GDN_CHUNK, GDN_HEAD_DIM
    assert 2 * c == hd
    tt = q_ref.shape[0]
    ri = lax.broadcasted_iota(jnp.int32, (c, hd), 0)
    lane = lax.broadcasted_iota(jnp.int32, (c, hd), 1)
    left = lane < c
    cj = jnp.where(left, lane, lane - c)
    ahead = jnp.where(left, ri - cj, cj - ri)
    incl = ahead >= 0
    strict = ahead > 0
    eye2 = (ri == cj).astype(F32)
    gates = _gates(ba_ref[...], alog_ref[...], dtb_ref[...], heads)

    def blockdiag(pk):
        zero = jnp.zeros_like(pk)
        return jnp.concatenate([jnp.where(left, pk, zero), jnp.where(left, zero, pk)], axis=0)

    def rows(ck):
        return slice(ck * c, (ck + 1) * c)

    def cols(h):
        return slice(h * hd, (h + 1) * hd)

    groups = [(ck, h) for ck in range(tt // c) for h in range(heads)]
    ng = range(len(groups))
    g_ck = [gates[rows(ck), :] for ck in range(tt // c)]
    gt_ck = [jnp.concatenate([g, g], axis=0).T for g in g_ck]

    q = [q_ref[rows(ck), cols(h)].astype(F32) for ck, h in groups]
    k = [k_ref[rows(ck), cols(h)].astype(F32) for ck, h in groups]
    v = [v_ref[rows(ck), cols(h)].astype(F32) for ck, h in groups]
    col = lambda ck, l: g_ck[ck][:, l:l + 1]
    beta = [(col(ck, h), col(ck, heads + h)) for ck, h in groups]
    gc = [(col(ck, 2 * heads + h), col(ck, 3 * heads + h)) for ck, h in groups]
    beta2 = [jnp.where(left, beta[gi][0], beta[gi][1]) for gi in ng]
    gc2 = [jnp.where(left, gc[gi][0], gc[gi][1]) for gi in ng]
    grow2 = [jnp.where(left[0:1], gt_ck[ck][2 * heads + h:2 * heads + h + 1, :],
                       gt_ck[ck][3 * heads + h:3 * heads + h + 1, :]) for ck, h in groups]
    decay = [jnp.exp(jnp.where(incl, gc2[gi] - grow2[gi], -jnp.inf)) for gi in ng]
    gram = [_dot_nt(jnp.concatenate([q[gi], k[gi]], axis=0).astype(BF16),
                    jnp.concatenate([k[gi], k[gi]], axis=0).astype(BF16)) for gi in ng]
    attn = [(gram[gi][:c] * decay[gi]).astype(BF16) for gi in ng]
    p = [-jnp.where(strict, gram[gi][c:] * beta2[gi] * decay[gi], 0.0) for gi in ng]
    inv = [eye2 + p[gi] for gi in ng]
    n_sq = int(math.log2(c)) - 1
    pbd = [blockdiag(p[gi].astype(BF16)) for gi in ng]
    p = [_dot(p[gi].astype(BF16), pbd[gi]) for gi in ng]
    for _ in range(n_sq - 1):
        pbd = [blockdiag(p[gi].astype(BF16)) for gi in ng]
        both = [_dot(jnp.concatenate([p[gi], inv[gi]], axis=0).astype(BF16), pbd[gi]) for gi in ng]
        p = [both[gi][:c] for gi in ng]
        inv = [inv[gi] + both[gi][c:] for gi in ng]
    inv = [inv[gi] + _dot(inv[gi].astype(BF16), blockdiag(p[gi].astype(BF16))) for gi in ng]

    eg = [(jnp.exp(gc[gi][0]), jnp.exp(gc[gi][1])) for gi in ng]
    g_last = [(gc[gi][0][c - 1:c, :], gc[gi][1][0:1, :]) for gi in ng]
    rhs = [jnp.concatenate([jnp.concatenate([v[gi] * beta[gi][d], (k[gi] * beta[gi][d]) * eg[gi][d]], axis=1)
                            for d in range(2)], axis=0).astype(BF16) for gi in ng]
    uw = [_dot(blockdiag(inv[gi].astype(BF16)), rhs[gi]) for gi in ng]
    for gi, (ck, h) in enumerate(groups):
        at_ref[rows(ck), cols(h)] = attn[gi]
        for d, (u_ref, wq_ref, kd_ref) in enumerate(((uf_ref, wqf_ref, kdf_ref), (ub_ref, wqb_ref, kdb_ref))):
            u_ref[rows(ck), cols(h)] = uw[gi][d * c:(d + 1) * c, :hd]
            wq_ref[ck, :, cols(h)] = jnp.concatenate([uw[gi][d * c:(d + 1) * c, hd:], q[gi] * eg[gi][d]],
                                                     axis=0).astype(BF16)
            kd_ref[rows(ck), cols(h)] = (k[gi] * jnp.exp(g_last[gi][d] - gc[gi][d])).astype(BF16)
            cd_ref[ck, d * heads + h:d * heads + h + 1, :] = jnp.broadcast_to(jnp.exp(g_last[gi][d]), (1, hd))


def _gdn_intra_call(q, k, v, ba, alog_row, dtb_row, layer, tt):
    b, t, w = q.shape
    heads = w // GDN_HEAD_DIM
    c, hd = GDN_CHUNK, GDN_HEAD_DIM
    n, nck = t // c, tt // c
    tok = lambda width: pl.BlockSpec((None, tt, width), lambda bi, ti: (bi, ti, 0))
    lay = pl.BlockSpec((None, 1, ba.shape[-1]), lambda bi, ti: (layer, 0, 0))
    wq_spec = pl.BlockSpec((None, nck, 2 * c, w), lambda bi, ti: (bi, ti, 0, 0))
    cd_spec = pl.BlockSpec((None, nck, 2 * heads, hd), lambda bi, ti: (bi, ti, 0, 0))
    tok_sds = lambda dt: jax.ShapeDtypeStruct((b, t, w), dt)
    wq_sds = jax.ShapeDtypeStruct((b, n, 2 * c, w), BF16)
    return pl.pallas_call(
        functools.partial(_gdn_intra_kernel, heads=heads),
        grid=(b, t // tt),
        in_specs=[tok(w), tok(w), tok(w), tok(ba.shape[-1]), lay, lay],
        out_specs=[tok(w), tok(w), wq_spec, wq_spec, tok(w), tok(w), tok(w), cd_spec],
        out_shape=[tok_sds(F32), tok_sds(F32), wq_sds, wq_sds, tok_sds(BF16), tok_sds(BF16), tok_sds(BF16),
                   jax.ShapeDtypeStruct((b, n, 2 * heads, hd), F32)],
        compiler_params=_params("parallel", "parallel"),
        name="gdn_intra",
    )(q, k, v, ba, alog_row, dtb_row)


def _gdn_scan_kernel(uf, wqf, kdf, atf, cdf, ub, wqb, kdb, atb, cdb, s0_ref, of_ref, ob_ref, sfin_ref, s_scr, *,
                     heads, bb):
    i = pl.program_id(1)
    n = pl.num_programs(1)
    c, hd = GDN_CHUNK, GDN_HEAD_DIM

    @pl.when(i == 0)
    def _():
        s_scr[...] = s0_ref[...]

    dirs = ((uf, wqf, kdf, atf, cdf, of_ref), (ub, wqb, kdb, atb, cdb, ob_ref))
    chains = [(bi, d, h) for bi in range(bb) for d in range(2) for h in range(heads)]
    cols = lambda h: slice(h * hd, (h + 1) * hd)
    zeros = jnp.zeros((c, hd), BF16)
    s_prev = {ch: s_scr[ch] for ch in chains}
    ws = {(bi, d, h): _dot(dirs[d][1][bi, :, cols(h)], s_prev[bi, d, h].astype(BF16)) for bi, d, h in chains}
    v_new = {(bi, d, h): (dirs[d][0][bi, :, cols(h)] - ws[bi, d, h][:c]).astype(BF16) for bi, d, h in chains}
    v_pad = {(bi, d, h): jnp.concatenate([v_new[bi, d, h], zeros] if d == 0 else [zeros, v_new[bi, d, h]], axis=0)
             for bi, d, h in chains}
    o = {(bi, d, h): ws[bi, d, h][c:] + _dot(dirs[d][3][bi, :, cols(h)], v_pad[bi, d, h]) for bi, d, h in chains}
    for bi, d, h in chains:
        r = d * heads + h
        s_scr[bi, d, h] = (s_prev[bi, d, h] * dirs[d][4][bi, r:r + 1, :]
                           + _dot_tn(dirs[d][2][bi, :, cols(h)], v_new[bi, d, h]))
        dirs[d][5][bi, :, cols(h)] = o[bi, d, h].astype(dirs[d][5].dtype)

    @pl.when(i == n - 1)
    def _():
        sfin_ref[...] = s_scr[...]


def _gdn_scan_call(intra, s0):
    uf, ub, wqf, wqb, kdf, kdb, at, cd = intra
    b, t, w = uf.shape
    heads = w // GDN_HEAD_DIM
    c, hd = GDN_CHUNK, GDN_HEAD_DIM
    n = t // c
    bb = next(cand for cand in (4, 2, 1) if b % cand == 0)
    fwd = lambda bi, i: (bi, i, 0)
    bwd = lambda bi, i: (bi, n - 1 - i, 0)
    fwd4 = lambda bi, i: (bi, i, 0, 0)
    bwd4 = lambda bi, i: (bi, n - 1 - i, 0, 0)

    def specs(tok_idx, chunk_idx):
        return [pl.BlockSpec((bb, c, w), tok_idx), pl.BlockSpec((bb, None, 2 * c, w), chunk_idx),
                pl.BlockSpec((bb, c, w), tok_idx), pl.BlockSpec((bb, c, w), tok_idx),
                pl.BlockSpec((bb, None, 2 * heads, hd), chunk_idx)]

    sspec = pl.BlockSpec((bb, 2, heads, hd, hd), lambda bi, i: (bi, 0, 0, 0, 0))
    return pl.pallas_call(
        functools.partial(_gdn_scan_kernel, heads=heads, bb=bb),
        grid=(b // bb, n),
        in_specs=specs(fwd, fwd4) + specs(bwd, bwd4) + [sspec],
        out_specs=[pl.BlockSpec((bb, c, w), fwd), pl.BlockSpec((bb, c, w), bwd), sspec],
        out_shape=[jax.ShapeDtypeStruct((b, t, w), BF16), jax.ShapeDtypeStruct((b, t, w), BF16),
                   jax.ShapeDtypeStruct((b, 2, heads, hd, hd), F32)],
        scratch_shapes=[pltpu.VMEM((bb, 2, heads, hd, hd), F32)],
        compiler_params=_params("parallel", "arbitrary"),
        name="gdn_scan",
    )(uf, wqf, kdf, at, cd, ub, wqb, kdb, at, cd, s0)


def _post_kernel(na_ref, of_ref, ob_ref, z_ref, x_ref, mod_ref, nw_ref, wo_ref, g1_ref, b1_ref, w1_ref, w2_ref,
                 g2_ref, b2_ref, o_ref, *, alpha, n_split, n_sub):
    hd = GDN_HEAD_DIM
    na_w = na_ref.shape[-1]
    tm = x_ref.shape[0]
    sub = tm // n_sub
    groups = [slice(g * sub, (g + 1) * sub) for g in range(n_sub)]

    def gated(rs):
        o = of_ref[rs, :].astype(F32) + ob_ref[rs, :].astype(F32)
        z = z_ref[rs, :].astype(F32)
        parts = []
        for h in range(o.shape[-1] // hd):
            oh = o[:, h * hd:(h + 1) * hd]
            oh = oh * lax.rsqrt(jnp.mean(oh * oh, axis=-1, keepdims=True) + NORM_EPS) * nw_ref[...]
            parts.append(oh * _silu(z[:, h * hd:(h + 1) * hd]))
        return jnp.concatenate(parts, axis=1).astype(BF16)

    def mix(rs, gdn):
        y = _dot(na_ref[rs, :], wo_ref[0:na_w, :]) + _dot(gdn, wo_ref[na_w:, :])
        return _layer_norm(alpha * x_ref[rs, :] + mod_ref[2:3, :] * y, g1_ref[...], b1_ref[...])

    def mlp(x1):
        hb = (x1 * (1.0 + mod_ref[4:5, :]) + mod_ref[3:4, :]).astype(BF16)
        ck = w1_ref.shape[1] // n_split
        acc = None
        for j in range(n_split):
            a = jnp.maximum(_dot(hb, w1_ref[:, j * ck:(j + 1) * ck]), 0.0)
            part = _dot((a * a).astype(BF16), w2_ref[j * ck:(j + 1) * ck, :])
            acc = part if acc is None else acc + part
        return acc

    gdn = [gated(rs) for rs in groups]
    x1 = [mix(rs, gdn[g]) for g, rs in enumerate(groups)]
    acc = [mlp(x1[g]) for g in range(n_sub)]
    for g, rs in enumerate(groups):
        o_ref[rs, :] = _layer_norm(alpha * x1[g] + mod_ref[5:6, :] * acc[g], g2_ref[...], b2_ref[...])


def _post_call(na_out, o_f, o_b, z, x, mods, nw, w_out_b, ln1, w1_b, w2_b, ln2, layer, mod_row, tm, alpha):
    b, t, d = x.shape
    na_w, gw = na_out.shape[-1], o_f.shape[-1]
    dff = w1_b.shape[-1]
    mod_idx = (lambda bi, ti: (layer, bi, 0, 0)) if mod_row is None else (lambda bi, ti: (layer, mod_row, 0, 0))
    tok = lambda width: pl.BlockSpec((None, tm, width), lambda bi, ti: (bi, ti, 0))
    lay = lambda *shape: pl.BlockSpec((None,) + shape, lambda bi, ti: (layer,) + (0,) * len(shape),
                                      pipeline_mode=pl.Buffered(1))
    return pl.pallas_call(
        functools.partial(_post_kernel, alpha=alpha, n_split=4, n_sub=2 if tm >= 512 else 1),
        grid=(b, t // tm),
        in_specs=[tok(na_w), tok(gw), tok(gw), tok(gw), tok(d),
                  pl.BlockSpec((None, None, N_MOD, d), mod_idx),
                  lay(1, GDN_HEAD_DIM), lay(na_w + gw, d), lay(1, d), lay(1, d),
                  lay(d, dff), lay(dff, d), lay(1, d), lay(1, d)],
        out_specs=tok(d),
        out_shape=jax.ShapeDtypeStruct((b, t, d), F32),
        compiler_params=_params("parallel", "parallel"),
        name="post_mix_mlp",
    )(na_out, o_f, o_b, z, x, mods, nw, w_out_b, ln1[0], ln1[1], w1_b, w2_b, ln2[0], ln2[1])


def _token_tile(t, target):
    return target if t % target == 0 else t


def kernel(x, c, ctx, c_ctx, w_ada, b_ada, w_in, conv_w, a_log, dt_bias, gdn_norm_w, rpb, w_out, ln1_g, ln1_b,
           w_mlp1, w_mlp2, ln2_g, ln2_b):
    depth, d, _ = w_ada.shape
    b, t, _ = x.shape
    l = ctx.shape[1]
    gdn_heads = a_log.shape[-1]
    alpha = (2 * depth) ** 0.25
    assert b + 1 <= MOD_ROWS and t % GRID_W == 0 and t % GDN_CHUNK == 0 and l % GDN_CHUNK == 0

    cc = jnp.zeros((MOD_ROWS, d), F32).at[:b].set(c).at[b].set(c_ctx)
    mods = _ada_call(cc, w_ada, b_ada).reshape(depth, MOD_ROWS, N_MOD, d)
    w_in_b = jnp.pad(w_in, ((0, 0), (0, 0), (0, BA_PAD - 4 * gdn_heads))).astype(BF16)
    w_out_b = w_out.astype(BF16)
    w1_b = w_mlp1.astype(BF16)
    w2_b = w_mlp2.astype(BF16)
    gate_pad = ((0, 0), (0, 0), (2 * gdn_heads, BA_PAD - 4 * gdn_heads))
    alog_row = jnp.pad(a_log.reshape(depth, 1, 2 * gdn_heads), gate_pad)
    dtb_row = jnp.pad(dt_bias.reshape(depth, 1, 2 * gdn_heads), gate_pad)
    nw = gdn_norm_w.reshape(depth, 1, GDN_HEAD_DIM)
    ln1g, ln1b = ln1_g.reshape(depth, 1, d), ln1_b.reshape(depth, 1, d)
    ln2g, ln2b = ln2_g.reshape(depth, 1, d), ln2_b.reshape(depth, 1, d)
    rope_tabs = _rope_tables(t)
    win_r = min(NA_WIN_R, t // GRID_W)
    zeros_state = jnp.zeros((b, 2, gdn_heads, GDN_HEAD_DIM, GDN_HEAD_DIM), F32)

    tm_in = _token_tile(t, TM_IN_PROJ)
    tm_lat, tm_ctx = _token_tile(t, TM_POST), _token_tile(l, TM_CTX)
    ti_lat, ti_ctx = _token_tile(t, TT_INTRA), _token_tile(l, TT_INTRA)
    x_lat, x_ctx = x, ctx
    for layer in range(depth):
        na_l, ql, kl, vl, z_l, ba_l = _inproj_call(x_lat, mods, w_in_b, conv_w, rope_tabs, layer, None, tm_in)
        na_c, qc, kc, vc, z_c, ba_c = _inproj_call(x_ctx, mods, w_in_b, conv_w, None, layer, b, tm_ctx)

        na_out_l = _na_call(na_l, na_c, _na_bias_table(rpb[layer], win_r))

        of_c, ob_c, s_ctx = _gdn_scan_call(_gdn_intra_call(qc, kc, vc, ba_c, alog_row, dtb_row, layer, ti_ctx),
                                           zeros_state)
        of_l, ob_l, _ = _gdn_scan_call(_gdn_intra_call(ql, kl, vl, ba_l, alog_row, dtb_row, layer, ti_lat), s_ctx)

        x_lat = _post_call(na_out_l, of_l, ob_l, z_l, x_lat, mods, nw, w_out_b, (ln1g, ln1b), w1_b, w2_b,
                           (ln2g, ln2b), layer, None, tm_lat, alpha)
        if layer < depth - 1:
            na_out_c = _ctx_attn_call(na_c)
            x_ctx = _post_call(na_out_c, of_c, ob_c, z_c, x_ctx, mods, nw, w_out_b, (ln1g, ln1b), w1_b, w2_b,
                               (ln2g, ln2b), layer, b, tm_ctx, alpha)
    return x_lat
```

```python
import functools
import math

import jax
import jax.numpy as jnp
from jax import lax
from jax.experimental import pallas as pl
from jax.experimental.pallas import tpu as pltpu

GRID_W = 64
NA_HEAD_DIM = 64
NA_WIN_R = 8
NA_WIN_C = 16
GDN_HEAD_DIM = 128
GDN_CHUNK = 64
ROPE_BASE = 10000.0
N_MOD = 6
LN_EPS = 1e-5
NORM_EPS = 1e-6
MOD_ROWS = 16
BA_PAD = 128
VMEM_LIMIT = 56 * 1024 * 1024
TM_IN_PROJ = 1024
TM_POST = 512
TM_CTX = 256
TT_INTRA = 8 * GDN_CHUNK
TN_ADA = 1536

BF16 = jnp.bfloat16
F32 = jnp.float32


def _params(*sem):
    return pltpu.CompilerParams(dimension_semantics=sem, vmem_limit_bytes=VMEM_LIMIT)


def _dot(a, b):
    return jnp.dot(a, b, preferred_element_type=F32)


def _dot_nt(a, b):
    return lax.dot_general(a, b, (((1,), (1,)), ((), ())), preferred_element_type=F32)


def _dot_tn(a, b):
    return lax.dot_general(a, b, (((0,), (0,)), ((), ())), preferred_element_type=F32)


def _silu(x):
    return x * jax.nn.sigmoid(x)


def _softplus(x):
    return jnp.maximum(x, 0.0) + jnp.log1p(jnp.exp(-jnp.abs(x)))


def _layer_norm(x, g, b):
    mu = jnp.mean(x, axis=-1, keepdims=True)
    xc = x - mu
    var = jnp.mean(xc * xc, axis=-1, keepdims=True)
    return xc * lax.rsqrt(var + LN_EPS) * g + b


def _ada_kernel(c_ref, w_ref, b_ref, o_ref):
    o_ref[...] = _dot(_silu(c_ref[...]), w_ref[...]) + b_ref[...]


def _ada_call(cc, w_ada, b_ada):
    depth, d, n = w_ada.shape
    tn = TN_ADA if n % TN_ADA == 0 else n
    return pl.pallas_call(
        _ada_kernel,
        grid=(depth, n // tn),
        in_specs=[pl.BlockSpec((MOD_ROWS, d), lambda l, j: (0, 0)),
                  pl.BlockSpec((None, d, tn), lambda l, j: (l, 0, j)),
                  pl.BlockSpec((None, 1, tn), lambda l, j: (l, 0, j))],
        out_specs=pl.BlockSpec((None, MOD_ROWS, tn), lambda l, j: (l, 0, j)),
        out_shape=jax.ShapeDtypeStruct((depth, MOD_ROWS, n), F32),
        compiler_params=_params("parallel", "parallel"),
        name="ada_mod",
    )(cc, w_ada, b_ada.reshape(depth, 1, n))


def _softmax_pv(s_list, v_list):
    m = None
    for s in s_list:
        sm = jnp.max(s, axis=-1, keepdims=True)
        m = sm if m is None else jnp.maximum(m, sm)
    den = None
    acc = None
    for s, v in zip(s_list, v_list):
        p = jnp.exp(s - m)
        ps = jnp.sum(p, axis=-1, keepdims=True)
        den = ps if den is None else den + ps
        pv = _dot(p.astype(BF16), v)
        acc = pv if acc is None else acc + pv
    return acc / den


def _na_kernel(q_ref, k_ref, v_ref, kc_ref, vc_ref, bias_ref, o_ref, *, heads, rows, win_r, rq):
    n_band = win_r * GRID_W
    scale = NA_HEAD_DIM ** -0.5
    pairs = range(heads // 2)
    gq = GRID_W
    lane = lax.broadcasted_iota(jnp.int32, (gq, 2 * NA_HEAD_DIM), 1)
    psl = lambda pr: slice(2 * pr * NA_HEAD_DIM, 2 * (pr + 1) * NA_HEAD_DIM)

    def scores(j):
        r = pl.program_id(1) * rq + j
        row_start = jnp.clip(r - win_r // 2, 0, rows - win_r)
        koff = pl.multiple_of(row_start * GRID_W, GRID_W)
        variant = row_start - r + win_r - 1
        s_lat, s_ctx = [], []
        for pr in pairs:
            qp = q_ref[j * gq:(j + 1) * gq, psl(pr)] * scale
            zero = jnp.zeros_like(qp)
            qs = jnp.concatenate([jnp.where(lane < NA_HEAD_DIM, qp, zero),
                                  jnp.where(lane >= NA_HEAD_DIM, qp, zero)], axis=0)
            bias = bias_ref[variant, 2 * pr:2 * pr + 2].reshape(2 * gq, n_band)
            s_lat.append(_dot_nt(qs, k_ref[pl.ds(koff, n_band), psl(pr)]) + bias)
            s_ctx.append(_dot_nt(qs, kc_ref[:, psl(pr)]))
        return koff, s_lat, s_ctx

    def softmax(st):
        koff, s_lat, s_ctx = st
        m = [jnp.maximum(jnp.max(s_lat[pr], axis=-1, keepdims=True), jnp.max(s_ctx[pr], axis=-1, keepdims=True))
             for pr in pairs]
        p_lat = [jnp.exp(s_lat[pr] - m[pr]) for pr in pairs]
        p_ctx = [jnp.exp(s_ctx[pr] - m[pr]) for pr in pairs]
        den = [jnp.sum(p_lat[pr], axis=-1, keepdims=True) + jnp.sum(p_ctx[pr], axis=-1, keepdims=True)
               for pr in pairs]
        return koff, [p.astype(BF16) for p in p_lat], [p.astype(BF16) for p in p_ctx], den

    def pv(j, st):
        koff, p_lat, p_ctx, den = st
        outs = []
        for pr in pairs:
            acc = (_dot(p_lat[pr], v_ref[pl.ds(koff, n_band), psl(pr)]) + _dot(p_ctx[pr], vc_ref[:, psl(pr)]))
            acc = acc / den[pr]
            outs.append(jnp.where(lane < NA_HEAD_DIM, acc[:gq], acc[gq:]))
        o_ref[j * gq:(j + 1) * gq, :] = jnp.concatenate(outs, axis=-1).astype(o_ref.dtype)

    pending = scores(0)
    for j in range(rq):
        nxt = scores(j + 1) if j + 1 < rq else None
        pv(j, softmax(pending))
        pending = nxt


def _na_bias_table(rpb, win_r):
    cols = jnp.arange(GRID_W)
    col_start = jnp.clip(cols - NA_WIN_C // 2, 0, GRID_W - NA_WIN_C)
    col_in = (cols[None, :] >= col_start[:, None]) & (cols[None, :] < col_start[:, None] + NA_WIN_C)
    edge = GRID_W - NA_WIN_C
    rpb_pad = jnp.pad(rpb.astype(F32), ((0, 0), (0, 0), (edge, edge)), mode="edge")
    rpb_cols = jnp.stack([rpb_pad[:, :, GRID_W - 1 - q:2 * GRID_W - 1 - q] for q in range(GRID_W)], axis=2)
    rpb_cols = jnp.where(col_in[None, None], rpb_cols, -jnp.inf)
    h = rpb.shape[0]
    variants = []
    for dv in range(win_r):
        dr0 = dv + NA_WIN_R - win_r
        band = rpb_cols[:, dr0:dr0 + win_r]
        variants.append(jnp.transpose(band, (0, 2, 1, 3)).reshape(h, GRID_W, win_r * GRID_W))
    return jnp.stack(variants)


def _na_call(na_lat, na_ctx, bias_tab):
    b, t, w3 = na_lat.shape
    w = w3 // 3
    heads = w // NA_HEAD_DIM
    l = na_ctx.shape[1]
    rows = t // GRID_W
    win_r = min(NA_WIN_R, rows)
    n_band = win_r * GRID_W
    rq = next(cand for cand in (8, 4, 2, 1) if rows % cand == 0)
    return pl.pallas_call(
        functools.partial(_na_kernel, heads=heads, rows=rows, win_r=win_r, rq=rq),
        grid=(b, rows // rq),
        in_specs=[pl.BlockSpec((None, rq * GRID_W, w), lambda bi, r: (bi, r, 0)),
                  pl.BlockSpec((None, t, w), lambda bi, r: (bi, 0, 1)),
                  pl.BlockSpec((None, t, w), lambda bi, r: (bi, 0, 2)),
                  pl.BlockSpec((None, l, w), lambda bi, r: (bi, 0, 1)),
                  pl.BlockSpec((None, l, w), lambda bi, r: (bi, 0, 2)),
                  pl.BlockSpec((win_r, heads, GRID_W, n_band), lambda bi, r: (0, 0, 0, 0),
                               pipeline_mode=pl.Buffered(1))],
        out_specs=pl.BlockSpec((None, rq * GRID_W, w), lambda bi, r: (bi, r, 0)),
        out_shape=jax.ShapeDtypeStruct((b, t, w), BF16),
        compiler_params=_params("parallel", "arbitrary"),
        name="na_attn",
    )(na_lat, na_lat, na_lat, na_ctx, na_ctx, bias_tab)


def _ctx_attn_kernel(q_ref, k_ref, v_ref, o_ref, *, heads):
    scale = NA_HEAD_DIM ** -0.5
    outs = []
    for h in range(heads):
        sl = slice(h * NA_HEAD_DIM, (h + 1) * NA_HEAD_DIM)
        s = _dot_nt(q_ref[:, sl], k_ref[:, sl]) * scale
        outs.append(_softmax_pv([s], [v_ref[:, sl]]))
    o_ref[...] = jnp.concatenate(outs, axis=-1).astype(o_ref.dtype)


def _ctx_attn_call(na_ctx):
    b, l, w3 = na_ctx.shape
    w = w3 // 3
    return pl.pallas_call(
        functools.partial(_ctx_attn_kernel, heads=w // NA_HEAD_DIM),
        grid=(b,),
        in_specs=[pl.BlockSpec((None, l, w), lambda bi: (bi, 0, 0)),
                  pl.BlockSpec((None, l, w), lambda bi: (bi, 0, 1)),
                  pl.BlockSpec((None, l, w), lambda bi: (bi, 0, 2))],
        out_specs=pl.BlockSpec((None, l, w), lambda bi: (bi, 0, 0)),
        out_shape=jax.ShapeDtypeStruct((b, l, w), BF16),
        compiler_params=_params("parallel"),
        name="ctx_attn",
    )(na_ctx, na_ctx, na_ctx)


HALO = 16


def _l2norm(x):
    return x * lax.rsqrt(jnp.sum(x * x, axis=-1, keepdims=True) + NORM_EPS)


def _rope(x, cos, sin):
    quarter = GDN_HEAD_DIM // 4
    lane = lax.broadcasted_iota(jnp.int32, x.shape, 1)
    first = (lane % (2 * quarter)) < quarter
    partner = jnp.where(first, pltpu.roll(x, GDN_HEAD_DIM - quarter, 1), pltpu.roll(x, quarter, 1))
    return x * cos + partner * sin


def _inproj_kernel(*refs, na3, g3, zw, use_rope):
    xp_ref, x_ref, xn_ref, mod_ref, w_ref, cw_ref = refs[:6]
    if use_rope:
        cos_ref, sin_ref = refs[6:8]
    na_ref, q_ref, k_ref, v_ref, z_ref, ba_ref, g_scr = refs[-7:]
    ti = pl.program_id(1)
    nt = pl.num_programs(1)
    tm = x_ref.shape[0]
    hd = GDN_HEAD_DIM
    scale, shift = 1.0 + mod_ref[1:2, :], mod_ref[0:1, :]
    modulate = lambda ref: (ref[...] * scale + shift).astype(BF16)
    hb = modulate(x_ref)
    h_prev = jnp.where(ti > 0, modulate(xp_ref), jnp.zeros((HALO, x_ref.shape[1]), BF16))
    h_next = jnp.where(ti < nt - 1, modulate(xn_ref), jnp.zeros((HALO, x_ref.shape[1]), BF16))
    h_ext = jnp.concatenate([h_prev, hb, h_next], axis=0)

    kw = cw_ref.shape[0]
    half = kw // 2
    heads = g3 // 3 // hd
    outs = (q_ref, k_ref, v_ref)
    mxu_w = 2 * hd
    for slab in range(g3 // mxu_w):
        g_slab = _dot(h_ext, w_ref[:, na3 + slab * mxu_w:na3 + (slab + 1) * mxu_w])
        for sub in range(mxu_w // hd):
            blk = slab * (mxu_w // hd) + sub
            part, h = blk // heads, blk % heads
            cs = slice(blk * hd, (blk + 1) * hd)
            g_scr[blk] = g_slab[:, sub * hd:(sub + 1) * hd]
            acc = None
            for j in range(kw):
                term = g_scr[blk, HALO - half + j:HALO - half + j + tm, :] * cw_ref[j:j + 1, cs]
                acc = term if acc is None else acc + term
            y = _silu(acc)
            if part < 2:
                y = _l2norm(y)
                if use_rope:
                    y = _rope(y, cos_ref[...], sin_ref[...])
            if part == 0:
                y = y * (hd ** -0.5)
            outs[part][:, h * hd:(h + 1) * hd] = y.astype(outs[part].dtype)
    na_ref[...] = _dot(hb, w_ref[:, 0:na3]).astype(na_ref.dtype)
    z_ref[...] = _dot(hb, w_ref[:, na3 + g3:na3 + g3 + zw]).astype(z_ref.dtype)
    ba_ref[...] = _dot(hb, w_ref[:, na3 + g3 + zw:])


def _inproj_call(x, mods, w_in_b, conv_w, rope_tabs, layer, mod_row, tm):
    b, t, d = x.shape
    na3, g3, zw = 3 * (d // 2), 3 * (d // 2), d // 2
    ncols = w_in_b.shape[-1]
    kw = conv_w.shape[1]
    use_rope = rope_tabs is not None
    th, nh = tm // HALO, t // HALO
    mod_idx = (lambda bi, ti: (layer, bi, 0, 0)) if mod_row is None else (lambda bi, ti: (layer, mod_row, 0, 0))
    tok = lambda width: pl.BlockSpec((None, tm, width), lambda bi, ti: (bi, ti, 0))
    lay = lambda *shape: pl.BlockSpec((None,) + shape, lambda bi, ti: (layer,) + (0,) * len(shape),
                                      pipeline_mode=pl.Buffered(1))
    in_specs = [pl.BlockSpec((None, HALO, d), lambda bi, ti: (bi, jnp.maximum(ti * th - 1, 0), 0)),
                tok(d),
                pl.BlockSpec((None, HALO, d), lambda bi, ti: (bi, jnp.minimum((ti + 1) * th, nh - 1), 0)),
                pl.BlockSpec((None, None, N_MOD, d), mod_idx), lay(d, ncols), lay(kw, g3)]
    args = [x, x, x, mods, w_in_b, conv_w]
    if use_rope:
        in_specs += [pl.BlockSpec((tm, GDN_HEAD_DIM), lambda bi, ti: (ti, 0))] * 2
        args += list(rope_tabs)
    return pl.pallas_call(
        functools.partial(_inproj_kernel, na3=na3, g3=g3, zw=zw, use_rope=use_rope),
        grid=(b, t // tm),
        in_specs=in_specs,
        out_specs=[tok(na3), tok(g3 // 3), tok(g3 // 3), tok(g3 // 3), tok(zw), tok(BA_PAD)],
        out_shape=[jax.ShapeDtypeStruct((b, t, na3), BF16)] + [jax.ShapeDtypeStruct((b, t, g3 // 3), BF16)] * 3
        + [jax.ShapeDtypeStruct((b, t, zw), BF16), jax.ShapeDtypeStruct((b, t, BA_PAD), F32)],
        scratch_shapes=[pltpu.VMEM((g3 // GDN_HEAD_DIM, tm + 2 * HALO, GDN_HEAD_DIM), F32)],
        compiler_params=_params("parallel", "parallel"),
        name="in_proj_conv",
    )(*args)


def _rope_tables(t):
    pos = jnp.arange(t)
    row = (pos // GRID_W).astype(F32)
    col = (pos % GRID_W).astype(F32)
    axis_dim = GDN_HEAD_DIM // 2
    inv_freq = ROPE_BASE ** (-jnp.arange(0, axis_dim, 2, dtype=F32) / axis_dim)
    ang_r = row[:, None] * inv_freq[None, :]
    ang_c = col[:, None] * inv_freq[None, :]
    cos = jnp.concatenate([jnp.cos(ang_r), jnp.cos(ang_r), jnp.cos(ang_c), jnp.cos(ang_c)], axis=-1)
    sin = jnp.concatenate([-jnp.sin(ang_r), jnp.sin(ang_r), -jnp.sin(ang_c), jnp.sin(ang_c)], axis=-1)
    return cos, sin


def _gates(x, alog, dtb, heads):
    tt = x.shape[0]
    c = GDN_CHUNK
    lane = lax.broadcasted_iota(jnp.int32, x.shape, 1)
    row = lax.broadcasted_iota(jnp.int32, x.shape, 0) % c
    beta = jax.nn.sigmoid(x)
    g = -jnp.exp(alog) * _softplus(x + dtb)
    pre, suf = g, g
    s = 1
    while s < c:
        pre = pre + jnp.where(row >= s, pltpu.roll(pre, s, 0), 0.0)
        suf = suf + jnp.where(row < c - s, pltpu.roll(suf, tt - s, 0), 0.0)
        s *= 2
    gc = jnp.where(lane < 3 * heads, pre, suf)
    return jnp.where(lane < 2 * heads, beta, gc)


def _gdn_intra_kernel(q_ref, k_ref, v_ref, ba_ref, alog_ref, dtb_ref, uf_ref, ub_ref, wqf_ref, wqb_ref, kdf_ref,
                      kdb_ref, at_ref, cd_ref, *, heads):
    c, hd = GDN_CHUNK, GDN_HEAD_DIM
    assert 2 * c == hd
    tt = q_ref.shape[0]
    ri = lax.broadcasted_iota(jnp.int32, (c, hd), 0)
    lane = lax.broadcasted_iota(jnp.int32, (c, hd), 1)
    left = lane < c
    cj = jnp.where(left, lane, lane - c)
    ahead = jnp.where(left, ri - cj, cj - ri)
    incl = ahead >= 0
    strict = ahead > 0
    eye2 = (ri == cj).astype(F32)
    gates = _gates(ba_ref[...], alog_ref[...], dtb_ref[...], heads)

    def blockdiag(pk):
        zero = jnp.zeros_like(pk)
        return jnp.concatenate([jnp.where(left, pk, zero), jnp.where(left, zero, pk)], axis=0)

    def rows(ck):
        return slice(ck * c, (ck + 1) * c)

    def cols(h):
        return slice(h * hd, (h + 1) * hd)

    groups = [(ck, h) for ck in range(tt // c) for h in range(heads)]
    ng = range(len(groups))
    g_ck = [gates[rows(ck), :] for ck in range(tt // c)]
    gt_ck = [jnp.concatenate([g, g], axis=0).T for g in g_ck]

    q = [q_ref[rows(ck), cols(h)].astype(F32) for ck, h in groups]
    k = [k_ref[rows(ck), cols(h)].astype(F32) for ck, h in groups]
    v = [v_ref[rows(ck), cols(h)].astype(F32) for ck, h in groups]
    col = lambda ck, l: g_ck[ck][:, l:l + 1]
    beta = [(col(ck, h), col(ck, heads + h)) for ck, h in groups]
    gc = [(col(ck, 2 * heads + h), col(ck, 3 * heads + h)) for ck, h in groups]
    beta2 = [jnp.where(left, beta[gi][0], beta[gi][1]) for gi in ng]
    gc2 = [jnp.where(left, gc[gi][0], gc[gi][1]) for gi in ng]
    grow2 = [jnp.where(left[0:1], gt_ck[ck][2 * heads + h:2 * heads + h + 1, :],
                       gt_ck[ck][3 * heads + h:3 * heads + h + 1, :]) for ck, h in groups]
    decay = [jnp.exp(jnp.where(incl, gc2[gi] - grow2[gi], -jnp.inf)) for gi in ng]
    gram = [_dot_nt(jnp.concatenate([q[gi], k[gi]], axis=0).astype(BF16),
                    jnp.concatenate([k[gi], k[gi]], axis=0).astype(BF16)) for gi in ng]
    attn = [(gram[gi][:c] * decay[gi]).astype(BF16) for gi in ng]
    p = [-jnp.where(strict, gram[gi][c:] * beta2[gi] * decay[gi], 0.0) for gi in ng]
    inv = [eye2 + p[gi] for gi in ng]
    n_sq = int(math.log2(c)) - 1
    pbd = [blockdiag(p[gi].astype(BF16)) for gi in ng]
    p = [_dot(p[gi].astype(BF16), pbd[gi]) for gi in ng]
    for _ in range(n_sq - 1):
        pbd = [blockdiag(p[gi].astype(BF16)) for gi in ng]
        both = [_dot(jnp.concatenate([p[gi], inv[gi]], axis=0).astype(BF16), pbd[gi]) for gi in ng]
        p = [both[gi][:c] for gi in ng]
        inv = [inv[gi] + both[gi][c:] for gi in ng]
    inv = [inv[gi] + _dot(inv[gi].astype(BF16), blockdiag(p[gi].astype(BF16))) for gi in ng]

    eg = [(jnp.exp(gc[gi][0]), jnp.exp(gc[gi][1])) for gi in ng]
    g_last = [(gc[gi][0][c - 1:c, :], gc[gi][1][0:1, :]) for gi in ng]
    rhs = [jnp.concatenate([jnp.concatenate([v[gi] * beta[gi][d], (k[gi] * beta[gi][d]) * eg[gi][d]], axis=1)
                            for d in range(2)], axis=0).astype(BF16) for gi in ng]
    uw = [_dot(blockdiag(inv[gi].astype(BF16)), rhs[gi]) for gi in ng]
    for gi, (ck, h) in enumerate(groups):
        at_ref[rows(ck), cols(h)] = attn[gi]
        for d, (u_ref, wq_ref, kd_ref) in enumerate(((uf_ref, wqf_ref, kdf_ref), (ub_ref, wqb_ref, kdb_ref))):
            u_ref[rows(ck), cols(h)] = uw[gi][d * c:(d + 1) * c, :hd]
            wq_ref[ck, :, cols(h)] = jnp.concatenate([uw[gi][d * c:(d + 1) * c, hd:], q[gi] * eg[gi][d]],
                                                     axis=0).astype(BF16)
            kd_ref[rows(ck), cols(h)] = (k[gi] * jnp.exp(g_last[gi][d] - gc[gi][d])).astype(BF16)
            cd_ref[ck, d * heads + h:d * heads + h + 1, :] = jnp.broadcast_to(jnp.exp(g_last[gi][d]), (1, hd))


def _gdn_intra_call(q, k, v, ba, alog_row, dtb_row, layer, tt):
    b, t, w = q.shape
    heads = w // GDN_HEAD_DIM
    c, hd = GDN_CHUNK, GDN_HEAD_DIM
    n, nck = t // c, tt // c
    tok = lambda width: pl.BlockSpec((None, tt, width), lambda bi, ti: (bi, ti, 0))
    lay = pl.BlockSpec((None, 1, ba.shape[-1]), lambda bi, ti: (layer, 0, 0))
    wq_spec = pl.BlockSpec((None, nck, 2 * c, w), lambda bi, ti: (bi, ti, 0, 0))
    cd_spec = pl.BlockSpec((None, nck, 2 * heads, hd), lambda bi, ti: (bi, ti, 0, 0))
    tok_sds = lambda dt: jax.ShapeDtypeStruct((b, t, w), dt)
    wq_sds = jax.ShapeDtypeStruct((b, n, 2 * c, w), BF16)
    return pl.pallas_call(
        functools.partial(_gdn_intra_kernel, heads=heads),
        grid=(b, t // tt),
        in_specs=[tok(w), tok(w), tok(w), tok(ba.shape[-1]), lay, lay],
        out_specs=[tok(w), tok(w), wq_spec, wq_spec, tok(w), tok(w), tok(w), cd_spec],
        out_shape=[tok_sds(F32), tok_sds(F32), wq_sds, wq_sds, tok_sds(BF16), tok_sds(BF16), tok_sds(BF16),
                   jax.ShapeDtypeStruct((b, n, 2 * heads, hd), F32)],
        compiler_params=_params("parallel", "parallel"),
        name="gdn_intra",
    )(q, k, v, ba, alog_row, dtb_row)


def _gdn_scan_kernel(uf, wqf, kdf, atf, cdf, ub, wqb, kdb, atb, cdb, s0_ref, of_ref, ob_ref, sfin_ref, s_scr, *,
                     heads, bb):
    i = pl.program_id(1)
    n = pl.num_programs(1)
    c, hd = GDN_CHUNK, GDN_HEAD_DIM

    @pl.when(i == 0)
    def _():
        s_scr[...] = s0_ref[...]

    dirs = ((uf, wqf, kdf, atf, cdf, of_ref), (ub, wqb, kdb, atb, cdb, ob_ref))
    chains = [(bi, d, h) for bi in range(bb) for d in range(2) for h in range(heads)]
    cols = lambda h: slice(h * hd, (h + 1) * hd)
    zeros = jnp.zeros((c, hd), BF16)
    s_prev = {ch: s_scr[ch] for ch in chains}
    ws = {(bi, d, h): _dot(dirs[d][1][bi, :, cols(h)], s_prev[bi, d, h].astype(BF16)) for bi, d, h in chains}
    v_new = {(bi, d, h): (dirs[d][0][bi, :, cols(h)] - ws[bi, d, h][:c]).astype(BF16) for bi, d, h in chains}
    v_pad = {(bi, d, h): jnp.concatenate([v_new[bi, d, h], zeros] if d == 0 else [zeros, v_new[bi, d, h]], axis=0)
             for bi, d, h in chains}
    o = {(bi, d, h): ws[bi, d, h][c:] + _dot(dirs[d][3][bi, :, cols(h)], v_pad[bi, d, h]) for bi, d, h in chains}
    for bi, d, h in chains:
        r = d * heads + h
        s_scr[bi, d, h] = (s_prev[bi, d, h] * dirs[d][4][bi, r:r + 1, :]
                           + _dot_tn(dirs[d][2][bi, :, cols(h)], v_new[bi, d, h]))
        dirs[d][5][bi, :, cols(h)] = o[bi, d, h].astype(dirs[d][5].dtype)

    @pl.when(i == n - 1)
    def _():
        sfin_ref[...] = s_scr[...]


def _gdn_scan_call(intra, s0):
    uf, ub, wqf, wqb, kdf, kdb, at, cd = intra
    b, t, w = uf.shape
    heads = w // GDN_HEAD_DIM
    c, hd = GDN_CHUNK, GDN_HEAD_DIM
    n = t // c
    bb = next(cand for cand in (8, 4, 2, 1) if b % cand == 0)
    fwd = lambda bi, i: (bi, i, 0)
    bwd = lambda bi, i: (bi, n - 1 - i, 0)
    fwd4 = lambda bi, i: (bi, i, 0, 0)
    bwd4 = lambda bi, i: (bi, n - 1 - i, 0, 0)

    def specs(tok_idx, chunk_idx):
        return [pl.BlockSpec((bb, c, w), tok_idx), pl.BlockSpec((bb, None, 2 * c, w), chunk_idx),
                pl.BlockSpec((bb, c, w), tok_idx), pl.BlockSpec((bb, c, w), tok_idx),
                pl.BlockSpec((bb, None, 2 * heads, hd), chunk_idx)]

    sspec = pl.BlockSpec((bb, 2, heads, hd, hd), lambda bi, i: (bi, 0, 0, 0, 0))
    return pl.pallas_call(
        functools.partial(_gdn_scan_kernel, heads=heads, bb=bb),
        grid=(b // bb, n),
        in_specs=specs(fwd, fwd4) + specs(bwd, bwd4) + [sspec],
        out_specs=[pl.BlockSpec((bb, c, w), fwd), pl.BlockSpec((bb, c, w), bwd), sspec],
        out_shape=[jax.ShapeDtypeStruct((b, t, w), BF16), jax.ShapeDtypeStruct((b, t, w), BF16),
                   jax.ShapeDtypeStruct((b, 2, heads, hd, hd), F32)],
        scratch_shapes=[pltpu.VMEM((bb, 2, heads, hd, hd), F32)],
        compiler_params=_params("parallel", "arbitrary"),
        name="gdn_scan",
    )(uf, wqf, kdf, at, cd, ub, wqb, kdb, at, cd, s0)


def _post_kernel(na_ref, of_ref, ob_ref, z_ref, x_ref, mod_ref, nw_ref, wo_ref, g1_ref, b1_ref, w1_ref, w2_ref,
                 g2_ref, b2_ref, o_ref, *, alpha, n_split, n_sub):
    hd = GDN_HEAD_DIM
    na_w = na_ref.shape[-1]
    tm = x_ref.shape[0]
    sub = tm // n_sub
    groups = [slice(g * sub, (g + 1) * sub) for g in range(n_sub)]

    def gated(rs):
        o = of_ref[rs, :].astype(F32) + ob_ref[rs, :].astype(F32)
        z = z_ref[rs, :].astype(F32)
        parts = []
        for h in range(o.shape[-1] // hd):
            oh = o[:, h * hd:(h + 1) * hd]
            oh = oh * lax.rsqrt(jnp.mean(oh * oh, axis=-1, keepdims=True) + NORM_EPS) * nw_ref[...]
            parts.append(oh * _silu(z[:, h * hd:(h + 1) * hd]))
        return jnp.concatenate(parts, axis=1).astype(BF16)

    def mix(rs, gdn):
        y = _dot(na_ref[rs, :], wo_ref[0:na_w, :]) + _dot(gdn, wo_ref[na_w:, :])
        return _layer_norm(alpha * x_ref[rs, :] + mod_ref[2:3, :] * y, g1_ref[...], b1_ref[...])

    def mlp(x1):
        hb = (x1 * (1.0 + mod_ref[4:5, :]) + mod_ref[3:4, :]).astype(BF16)
        ck = w1_ref.shape[1] // n_split
        acc = None
        for j in range(n_split):
            a = jnp.maximum(_dot(hb, w1_ref[:, j * ck:(j + 1) * ck]), 0.0)
            part = _dot((a * a).astype(BF16), w2_ref[j * ck:(j + 1) * ck, :])
            acc = part if acc is None else acc + part
        return acc

    gdn = [gated(rs) for rs in groups]
    x1 = [mix(rs, gdn[g]) for g, rs in enumerate(groups)]
    acc = [mlp(x1[g]) for g in range(n_sub)]
    for g, rs in enumerate(groups):
        o_ref[rs, :] = _layer_norm(alpha * x1[g] + mod_ref[5:6, :] * acc[g], g2_ref[...], b2_ref[...])


def _post_call(na_out, o_f, o_b, z, x, mods, nw, w_out_b, ln1, w1_b, w2_b, ln2, layer, mod_row, tm, alpha):
    b, t, d = x.shape
    na_w, gw = na_out.shape[-1], o_f.shape[-1]
    dff = w1_b.shape[-1]
    mod_idx = (lambda bi, ti: (layer, bi, 0, 0)) if mod_row is None else (lambda bi, ti: (layer, mod_row, 0, 0))
    tok = lambda width: pl.BlockSpec((None, tm, width), lambda bi, ti: (bi, ti, 0))
    lay = lambda *shape: pl.BlockSpec((None,) + shape, lambda bi, ti: (layer,) + (0,) * len(shape),
                                      pipeline_mode=pl.Buffered(1))
    return pl.pallas_call(
        functools.partial(_post_kernel, alpha=alpha, n_split=4, n_sub=2 if tm >= 512 else 1),
        grid=(b, t // tm),
        in_specs=[tok(na_w), tok(gw), tok(gw), tok(gw), tok(d),
                  pl.BlockSpec((None, None, N_MOD, d), mod_idx),
                  lay(1, GDN_HEAD_DIM), lay(na_w + gw, d), lay(1, d), lay(1, d),
                  lay(d, dff), lay(dff, d), lay(1, d), lay(1, d)],
        out_specs=tok(d),
        out_shape=jax.ShapeDtypeStruct((b, t, d), F32),
        compiler_params=_params("parallel", "parallel"),
        name="post_mix_mlp",
    )(na_out, o_f, o_b, z, x, mods, nw, w_out_b, ln1[0], ln1[1], w1_b, w2_b, ln2[0], ln2[1])


def _token_tile(t, target):
    return target if t % target == 0 else t


def kernel(x, c, ctx, c_ctx, w_ada, b_ada, w_in, conv_w, a_log, dt_bias, gdn_norm_w, rpb, w_out, ln1_g, ln1_b,
           w_mlp1, w_mlp2, ln2_g, ln2_b):
    depth, d, _ = w_ada.shape
    b, t, _ = x.shape
    l = ctx.shape[1]
    gdn_heads = a_log.shape[-1]
    alpha = (2 * depth) ** 0.25
    assert b + 1 <= MOD_ROWS and t % GRID_W == 0 and t % GDN_CHUNK == 0 and l % GDN_CHUNK == 0

    cc = jnp.zeros((MOD_ROWS, d), F32).at[:b].set(c).at[b].set(c_ctx)
    mods = _ada_call(cc, w_ada, b_ada).reshape(depth, MOD_ROWS, N_MOD, d)
    w_in_b = jnp.pad(w_in, ((0, 0), (0, 0), (0, BA_PAD - 4 * gdn_heads))).astype(BF16)
    w_out_b = w_out.astype(BF16)
    w1_b = w_mlp1.astype(BF16)
    w2_b = w_mlp2.astype(BF16)
    gate_pad = ((0, 0), (0, 0), (2 * gdn_heads, BA_PAD - 4 * gdn_heads))
    alog_row = jnp.pad(a_log.reshape(depth, 1, 2 * gdn_heads), gate_pad)
    dtb_row = jnp.pad(dt_bias.reshape(depth, 1, 2 * gdn_heads), gate_pad)
    nw = gdn_norm_w.reshape(depth, 1, GDN_HEAD_DIM)
    ln1g, ln1b = ln1_g.reshape(depth, 1, d), ln1_b.reshape(depth, 1, d)
    ln2g, ln2b = ln2_g.reshape(depth, 1, d), ln2_b.reshape(depth, 1, d)
    rope_tabs = _rope_tables(t)
    win_r = min(NA_WIN_R, t // GRID_W)
    zeros_state = jnp.zeros((b, 2, gdn_heads, GDN_HEAD_DIM, GDN_HEAD_DIM), F32)

    tm_in = _token_tile(t, TM_IN_PROJ)
    tm_lat, tm_ctx = _token_tile(t, TM_POST), _token_tile(l, TM_CTX)
    ti_lat, ti_ctx = _token_tile(t, TT_INTRA), _token_tile(l, TT_INTRA)
    x_lat, x_ctx = x, ctx
    for layer in range(depth):
        na_l, ql, kl, vl, z_l, ba_l = _inproj_call(x_lat, mods, w_in_b, conv_w, rope_tabs, layer, None, tm_in)
        na_c, qc, kc, vc, z_c, ba_c = _inproj_call(x_ctx, mods, w_in_b, conv_w, None, layer, b, tm_ctx)

        na_out_l = _na_call(na_l, na_c, _na_bias_table(rpb[layer], win_r))

        of_c, ob_c, s_ctx = _gdn_scan_call(_gdn_intra_call(qc, kc, vc, ba_c, alog_row, dtb_row, layer, ti_ctx),
                                           zeros_state)
        of_l, ob_l, _ = _gdn_scan_call(_gdn_intra_call(ql, kl, vl, ba_l, alog_row, dtb_row, layer, ti_lat), s_ctx)

        x_lat = _post_call(na_out_l, of_l, ob_l, z_l, x_lat, mods, nw, w_out_b, (ln1g, ln1b), w1_b, w2_b,
                           (ln2g, ln2b), layer, None, tm_lat, alpha)
        if layer < depth - 1:
            na_out_c = _ctx_attn_call(na_c)
            x_ctx = _post_call(na_out_c, of_c, ob_c, z_c, x_ctx, mods, nw, w_out_b, (ln1g, ln1b), w1_b, w2_b,
                               (ln2g, ln2b), layer, b, tm_ctx, alpha)
    return x_lat
```
